```python
import math
import jax
import jax.numpy as jnp
from jax import lax
import numpy as np

D_MODEL = 2048
BATCH = 2
SEQ = 4096
DEPTH = 2

CHUNK = 64
N_EVEN = (DEPTH + 1) // 2
N_ODD = DEPTH // 2
ALPHA = (2.0 * DEPTH) ** 0.25
BETA = (8.0 * DEPTH) ** -0.25
LN_EPS = 1e-5
NEG = -1e30

A_WIDTH = D_MODEL // 2
A_BLOCK = 128
A_GROUP_CH = 128
A_GROUPS = A_WIDTH // A_GROUP_CH
B_WIDTH = D_MODEL - A_WIDTH
DIFF_HEAD_DIM = 64
DIFF_V_DIM = 2 * DIFF_HEAD_DIM
DIFF_HEADS = B_WIDTH // DIFF_V_DIM
Q_BLOCK = 128
EVEN_IN = 2 * A_WIDTH + 3 * B_WIDTH
POOL_WINDOWS = (2, 4, 8, 16)
POOL_GROUP_CH = D_MODEL // len(POOL_WINDOWS)
N_MEM = 256
XA_HEADS = 4
XA_HEAD_DIM = 128
XA_WIDTH = XA_HEADS * XA_HEAD_DIM
N_EXPERTS = 32
TOP_K = 4
D_FF = D_MODEL
SWIGLU_LIMIT = 7.0
SWIGLU_ALPHA = 1.702
MOE_BLOCK = 256

kernel_name = 'hybrid_gmlp_diffattn_pool_moe_deepnorm'


def _layernorm(x, g, b):
    xf = x.astype(jnp.float32)
    mu = xf.mean(-1, keepdims=True)
    var = jnp.square(xf - mu).mean(-1, keepdims=True)
    return ((xf - mu) * lax.rsqrt(var + LN_EPS) * g + b).astype(x.dtype)


def _rmsnorm(x, g):
    xf = x.astype(jnp.float32)
    return (xf * lax.rsqrt(jnp.square(xf).mean(-1, keepdims=True) + LN_EPS) * g).astype(x.dtype)


def _alibi_slopes(n):
    return jnp.asarray(2.0 ** (-8.0 * np.arange(1, n + 1) / n), dtype=jnp.float32)


def _spatial_gating(u, v, ln_g, ln_b, w_s, b_s):
    bsz, seq, _ = v.shape
    v = _layernorm(v, ln_g, ln_b)
    vr = v.reshape(bsz, seq // A_BLOCK, A_BLOCK, A_GROUPS, A_GROUP_CH)
    cid = jnp.arange(A_BLOCK) // CHUNK
    mask = cid[None, :] <= cid[:, None]
    w = jnp.where(mask[None], w_s, 0.0)
    z = jnp.einsum('gts,bnsgc->bntgc', w, vr) + b_s.T[None, None, :, :, None]
    return u * z.reshape(bsz, seq, A_WIDTH)


def _diff_attention(q, k, v, lam, subln_g, lambda_init):
    bsz, seq, _ = q.shape
    nb = seq // Q_BLOCK
    q = q.reshape(bsz, nb, Q_BLOCK, DIFF_HEADS, 2, DIFF_HEAD_DIM).transpose(1, 0, 3, 4, 2, 5)
    k = k.reshape(bsz, seq, DIFF_HEADS, 2, DIFF_HEAD_DIM).transpose(0, 2, 3, 1, 4)
    v = v.reshape(bsz, seq, DIFF_HEADS, DIFF_V_DIM).transpose(0, 2, 1, 3)
    slopes = _alibi_slopes(DIFF_HEADS)
    k_pos = jnp.arange(seq)
    scale = DIFF_HEAD_DIM ** -0.5

    def block(args):
        qb, i = args
        q_pos = i * Q_BLOCK + jnp.arange(Q_BLOCK)
        s = jnp.einsum('bhmqd,bhmkd->bhmqk', qb, k).astype(jnp.float32) * scale
        dist = jnp.abs(q_pos[:, None] - k_pos[None, :]).astype(jnp.float32)
        allowed = (k_pos[None, :] // CHUNK) <= (q_pos[:, None] // CHUNK)
        s = jnp.where(allowed, s - slopes[:, None, None, None] * dist, NEG)
        p = jax.nn.softmax(s, axis=-1)
        a = p[:, :, 0] - lam * p[:, :, 1]
        return jnp.einsum('bhqk,bhkd->bhqd', a.astype(v.dtype), v)

    o = lax.map(block, (q, jnp.arange(nb)))
    o = o.transpose(1, 0, 3, 2, 4).reshape(bsz, seq, DIFF_HEADS, DIFF_V_DIM)
    o = _rmsnorm(o, subln_g) * (1.0 - lambda_init)
    return o.reshape(bsz, seq, B_WIDTH)


def _even_mixer(x, w_in, ln_v_g, ln_v_b, w_s, b_s, lq1, lk1, lq2, lk2, subln_g, w_out, lambda_init):
    h = x @ w_in
    uv, q, k, v = jnp.split(h, [2 * A_WIDTH, 2 * A_WIDTH + B_WIDTH, 2 * A_WIDTH + 2 * B_WIDTH], axis=-1)
    u, gv = jnp.split(jax.nn.gelu(uv, approximate=False), 2, axis=-1)
    a_out = _spatial_gating(u, gv, ln_v_g, ln_v_b, w_s, b_s)
    lam = (jnp.exp(jnp.sum(lq1.astype(jnp.float32) * lk1.astype(jnp.float32)))
           - jnp.exp(jnp.sum(lq2.astype(jnp.float32) * lk2.astype(jnp.float32))) + lambda_init)
    b_out = _diff_attention(q, k, v, lam, subln_g, lambda_init)
    return jnp.concatenate([a_out, b_out], axis=-1) @ w_out


def _trailing_mean_minus_self(h, window):
    seq = h.shape[1]
    hf = h.astype(jnp.float32)
    c = jnp.cumsum(hf, axis=1)
    lagged = jnp.pad(c, ((0, 0), (window, 0), (0, 0)))[:, :seq]
    count = jnp.minimum(jnp.arange(seq) + 1, window).astype(jnp.float32)[None, :, None]
    return ((c - lagged) / count - hf).astype(h.dtype)


def _odd_mixer(x, w_in, w_grp, scale, w_out):
    h = x @ w_in
    groups = jnp.split(h, len(POOL_WINDOWS), axis=-1)
    y = jnp.concatenate([_trailing_mean_minus_self(g, w) @ w_grp[j]
                         for j, (g, w) in enumerate(zip(groups, POOL_WINDOWS))], axis=-1)
    return (y * scale) @ w_out


def _cross_attention(x, mem, w_q, w_k, w_v, w_o):
    bsz, seq, _ = x.shape
    n_mem = mem.shape[1]
    q = (x @ w_q).reshape(bsz, seq, XA_HEADS, XA_HEAD_DIM)
    k = (mem @ w_k).reshape(bsz, n_mem, XA_HEADS, XA_HEAD_DIM)
    v = (mem @ w_v).reshape(bsz, n_mem, XA_HEADS, XA_HEAD_DIM)
    s = jnp.einsum('bqhd,bkhd->bhqk', q, k).astype(jnp.float32) * (XA_HEAD_DIM ** -0.5)
    p = jax.nn.softmax(s, axis=-1)
    o = jnp.einsum('bhqk,bkhd->bqhd', p.astype(v.dtype), v).reshape(bsz, seq, XA_WIDTH)
    return o @ w_o


def _moe(x, w_router, b_router, w_gu, b_gu, w_down, b_down):
    bsz, seq, d = x.shape
    xt = x.reshape(-1, d)
    n = xt.shape[0]
    logits = (xt @ w_router + b_router).astype(jnp.float32)
    top_val, top_idx = lax.top_k(logits, TOP_K)
    gates = jax.nn.softmax(top_val, axis=-1)
    n_assign = n * TOP_K
    e_flat = top_idx.reshape(-1)
    tok_flat = jnp.arange(n_assign) // TOP_K
    g_flat = gates.reshape(-1)
    order = jnp.argsort(e_flat)
    e_s, tok_s, g_s = e_flat[order], tok_flat[order], g_flat[order]
    counts = jnp.bincount(e_flat, length=N_EXPERTS)
    starts = jnp.cumsum(counts) - counts
    padded = (counts + MOE_BLOCK - 1) // MOE_BLOCK * MOE_BLOCK
    p_ends = jnp.cumsum(padded)
    p_starts = p_ends - padded
    dest = p_starts[e_s] + (jnp.arange(n_assign) - starts[e_s])
    n_rows = -(-n_assign // MOE_BLOCK) * MOE_BLOCK + N_EXPERTS * MOE_BLOCK
    n_blocks = n_rows // MOE_BLOCK
    block_e = jnp.minimum(jnp.searchsorted(p_ends, jnp.arange(n_blocks) * MOE_BLOCK, side='right'), N_EXPERTS - 1)
    x_rows = jnp.zeros((n_rows, d), x.dtype).at[dest].set(xt[tok_s])

    def expert_block(args):
        xb, e = args
        h = xb @ w_gu[e] + b_gu[e]
        glu, lin = jnp.split(h, 2, axis=-1)
        glu = jnp.minimum(glu, SWIGLU_LIMIT)
        lin = jnp.clip(lin, -SWIGLU_LIMIT, SWIGLU_LIMIT)
        act = glu * jax.nn.sigmoid(SWIGLU_ALPHA * glu) * (lin + 1.0)
        return act @ w_down[e] + b_down[e]

    y_rows = lax.map(expert_block, (x_rows.reshape(n_blocks, MOE_BLOCK, d), block_e)).reshape(n_rows, d)
    y = jax.ops.segment_sum(y_rows[dest] * g_s[:, None].astype(x.dtype), tok_s, num_segments=n)
    return y.reshape(bsz, seq, d)


def _init(key, shape, std):
    a = std * math.sqrt(3.0)
    return jax.random.uniform(key, shape, jnp.float32, -a, a)


def setup_inputs(seed: int = 0) -> dict:
    key = jax.random.key(seed)
    ks = jax.random.split(key, 32)
    d = D_MODEL
    f32 = jnp.float32

    def noise(k, shape, s):
        return s * jax.random.normal(k, shape, f32)

    return {
        'x': jax.random.normal(ks[0], (BATCH, SEQ, d), f32),
        'mem': jax.random.normal(ks[1], (BATCH, N_MEM, d), f32),
        'even_w_in': _init(ks[2], (N_EVEN, d, EVEN_IN), d ** -0.5),
        'even_ln_v_g': 1.0 + noise(ks[3], (N_EVEN, A_WIDTH), 0.02),
        'even_ln_v_b': noise(ks[4], (N_EVEN, A_WIDTH), 0.02),
        'even_w_s': _init(ks[5], (N_EVEN, A_GROUPS, A_BLOCK, A_BLOCK), A_BLOCK ** -0.5),
        'even_b_s': 1.0 + noise(ks[6], (N_EVEN, A_GROUPS, A_BLOCK), 0.02),
        'even_lam_q1': noise(ks[7], (N_EVEN, DIFF_HEAD_DIM), 0.1),
        'even_lam_k1': noise(ks[8], (N_EVEN, DIFF_HEAD_DIM), 0.1),
        'even_lam_q2': noise(ks[9], (N_EVEN, DIFF_HEAD_DIM), 0.1),
        'even_lam_k2': noise(ks[10], (N_EVEN, DIFF_HEAD_DIM), 0.1),
        'even_subln_g': 1.0 + noise(ks[11], (N_EVEN, DIFF_V_DIM), 0.02),
        'even_w_out': _init(ks[12], (N_EVEN, d, d), BETA * d ** -0.5),
        'odd_w_in': _init(ks[13], (N_ODD, d, d), d ** -0.5),
        'odd_w_grp': _init(ks[14], (N_ODD, len(POOL_WINDOWS), POOL_GROUP_CH, POOL_GROUP_CH), POOL_GROUP_CH ** -0.5),
        'odd_scale': 1.0 + noise(ks[15], (N_ODD, d), 0.02),
        'odd_w_out': _init(ks[16], (N_ODD, d, d), BETA * d ** -0.5),
        'xa_w_q': _init(ks[17], (DEPTH, d, XA_WIDTH), d ** -0.5),
        'xa_w_k': _init(ks[18], (DEPTH, d, XA_WIDTH), d ** -0.5),
        'xa_w_v': _init(ks[19], (DEPTH, d, XA_WIDTH), d ** -0.5),
        'xa_w_o': _init(ks[20], (DEPTH, XA_WIDTH, d), BETA * XA_WIDTH ** -0.5),
        'moe_w_router': _init(ks[21], (DEPTH, d, N_EXPERTS), d ** -0.5),
        'moe_b_router': noise(ks[22], (DEPTH, N_EXPERTS), 0.01),
        'moe_w_gu': _init(ks[23], (DEPTH, N_EXPERTS, d, 2 * D_FF), d ** -0.5),
        'moe_b_gu': noise(ks[24], (DEPTH, N_EXPERTS, 2 * D_FF), 0.02),
        'moe_w_down': _init(ks[25], (DEPTH, N_EXPERTS, D_FF, d), BETA * D_FF ** -0.5),
        'moe_b_down': noise(ks[26], (DEPTH, N_EXPERTS, d), 0.02),
        'ln_g': 1.0 + noise(ks[27], (DEPTH, 3, d), 0.02),
        'ln_b': noise(ks[28], (DEPTH, 3, d), 0.02),
    }


def reference(x, mem, even_w_in, even_ln_v_g, even_ln_v_b, even_w_s, even_b_s,
              even_lam_q1, even_lam_k1, even_lam_q2, even_lam_k2, even_subln_g, even_w_out,
              odd_w_in, odd_w_grp, odd_scale, odd_w_out,
              xa_w_q, xa_w_k, xa_w_v, xa_w_o,
              moe_w_router, moe_b_router, moe_w_gu, moe_b_gu, moe_w_down, moe_b_down,
              ln_g, ln_b):
    for l in range(DEPTH):
        i = l // 2
        if l % 2 == 0:
            lambda_init = 0.8 - 0.6 * math.exp(-0.3 * l)
            m = _even_mixer(x, even_w_in[i], even_ln_v_g[i], even_ln_v_b[i], even_w_s[i], even_b_s[i],
                            even_lam_q1[i], even_lam_k1[i], even_lam_q2[i], even_lam_k2[i],
                            even_subln_g[i], even_w_out[i], lambda_init)
        else:
            m = _odd_mixer(x, odd_w_in[i], odd_w_grp[i], odd_scale[i], odd_w_out[i])
        x = _layernorm(ALPHA * x + m, ln_g[l, 0], ln_b[l, 0])
        c = _cross_attention(x, mem, xa_w_q[l], xa_w_k[l], xa_w_v[l], xa_w_o[l])
        x = _layernorm(ALPHA * x + c, ln_g[l, 1], ln_b[l, 1])
        f = _moe(x, moe_w_router[l], moe_b_router[l], moe_w_gu[l], moe_b_gu[l], moe_w_down[l], moe_b_down[l])
        x = _layernorm(ALPHA * x + f, ln_g[l, 2], ln_b[l, 2])
    return x
```

```python
import functools
import math

import jax
import jax.numpy as jnp
from jax import lax
from jax.experimental import pallas as pl
from jax.experimental.pallas import tpu as pltpu

_F32 = jnp.float32
_BF16 = jnp.bfloat16

D_MODEL = 2048
DEPTH = 2
CHUNK = 64
ALPHA = (2.0 * DEPTH) ** 0.25
LN_EPS = 1e-5
NEG = -1e30
A_WIDTH = 1024
A_BLOCK = 128
A_GROUPS = 8
DIFF_HEAD_DIM = 64
DIFF_V_DIM = 128
DIFF_HEADS = 8
POOL_WINDOWS = (2, 4, 8, 16)
POOL_GROUP_CH = 512
POOL_HALO = 16
XA_HEADS = 4
XA_HEAD_DIM = 128
XA_WIDTH = 512
N_EXPERTS = 32
TOP_K = 4
D_FF = 2048
SWIGLU_LIMIT = 7.0
SWIGLU_ALPHA = 1.702
MOE_BLOCK = 256

LANES = 128
VMEM_LIMIT_BYTES = 56 * 1024 * 1024

MOE_SB_BLOCKS = 6
MOE_TF = 256
MOE_NF = D_FF // MOE_TF


def _params(sem, vmem=VMEM_LIMIT_BYTES):
    return pltpu.CompilerParams(dimension_semantics=sem, vmem_limit_bytes=vmem)


def _layernorm(y, g, b):
    mu = jnp.mean(y, axis=-1, keepdims=True)
    d = y - mu
    var = jnp.mean(d * d, axis=-1, keepdims=True)
    return d * lax.rsqrt(var + LN_EPS) * g + b


def _gelu(x):
    return 0.5 * x * (1.0 + lax.erf(x * (2.0 ** -0.5)))


def _mm_body(x_ref, w_ref, o_ref, xb_ref, *, gelu_tiles):
    j = pl.program_id(1)

    @pl.when(j == 0)
    def _():
        xb_ref[...] = x_ref[...].astype(_BF16)

    acc = jnp.dot(xb_ref[...], w_ref[...], preferred_element_type=_F32)
    if gelu_tiles:
        @pl.when(j < gelu_tiles)
        def _():
            o_ref[...] = _gelu(acc).astype(o_ref.dtype)

        @pl.when(j >= gelu_tiles)
        def _():
            o_ref[...] = acc.astype(o_ref.dtype)
    else:
        o_ref[...] = acc.astype(o_ref.dtype)


def _mm(x, w, *, tm, tn, out_dtype, gelu_cols=0, name):
    m, k = x.shape
    n = w.shape[1]
    assert m % tm == 0 and n % tn == 0 and gelu_cols % tn == 0
    return pl.pallas_call(
        functools.partial(_mm_body, gelu_tiles=gelu_cols // tn),
        out_shape=jax.ShapeDtypeStruct((m, n), out_dtype),
        grid=(m // tm, n // tn),
        in_specs=[pl.BlockSpec((tm, k), lambda i, j: (i, 0)),
                  pl.BlockSpec((k, tn), lambda i, j: (0, j))],
        out_specs=pl.BlockSpec((tm, tn), lambda i, j: (i, j)),
        scratch_shapes=[pltpu.VMEM((tm, k), _BF16)],
        compiler_params=_params(("parallel", "arbitrary")),
        name=name,
    )(x, w)


def _mm_ln_body(*refs, n_a):
    a_refs = refs[:n_a]
    w_ref, r_ref, g_ref, b_ref, o_ref = refs[n_a:]
    acc = None
    row = 0
    for a_ref in a_refs:
        kk = a_ref.shape[1]
        part = jnp.dot(a_ref[...], w_ref[row:row + kk, :], preferred_element_type=_F32)
        acc = part if acc is None else acc + part
        row += kk
    o_ref[...] = _layernorm(ALPHA * r_ref[...] + acc, g_ref[...], b_ref[...])


def _mm_ln(a_list, w, resid, g, b, *, tm, name):
    m = resid.shape[0]
    n = w.shape[1]
    in_specs = [pl.BlockSpec((tm, a.shape[1]), lambda i: (i, 0)) for a in a_list]
    in_specs += [pl.BlockSpec(w.shape, lambda i: (0, 0)),
                 pl.BlockSpec((tm, n), lambda i: (i, 0)),
                 pl.BlockSpec((1, n), lambda i: (0, 0)),
                 pl.BlockSpec((1, n), lambda i: (0, 0))]
    return pl.pallas_call(
        functools.partial(_mm_ln_body, n_a=len(a_list)),
        out_shape=jax.ShapeDtypeStruct((m, n), _F32),
        grid=(m // tm,),
        in_specs=in_specs,
        out_specs=pl.BlockSpec((tm, n), lambda i: (i, 0)),
        compiler_params=_params(("parallel",)),
        name=name,
    )(*a_list, w, resid, g, b)


def _sgu_body(u_ref, gv_ref, lng_ref, lnb_ref, ws_ref, bs_ref, o_ref):
    tb = u_ref.shape[0]
    v = _layernorm(gv_ref[...].astype(_F32), lng_ref[...], lnb_ref[...]).astype(_BF16)
    r = lax.broadcasted_iota(jnp.int32, (A_BLOCK, A_BLOCK), 0) // CHUNK
    c = lax.broadcasted_iota(jnp.int32, (A_BLOCK, A_BLOCK), 1) // CHUNK
    causal = c <= r
    for g in range(A_GROUPS):
        wm = jnp.where(causal, ws_ref[g], 0.0).astype(_BF16)
        cols = slice(g * A_BLOCK, (g + 1) * A_BLOCK)
        for n in range(tb // A_BLOCK):
            rows = slice(n * A_BLOCK, (n + 1) * A_BLOCK)
            z = jnp.dot(wm, v[rows, cols], preferred_element_type=_F32) + bs_ref[g]
            o_ref[rows, cols] = (u_ref[rows, cols].astype(_F32) * z).astype(o_ref.dtype)


def _spatial_gating(h, ln_g, ln_b, w_s, b_s, *, tb=512):
    n = h.shape[0]
    return pl.pallas_call(
        _sgu_body,
        out_shape=jax.ShapeDtypeStruct((n, A_WIDTH), _BF16),
        grid=(n // tb,),
        in_specs=[pl.BlockSpec((tb, A_WIDTH), lambda i: (i, 0)),
                  pl.BlockSpec((tb, A_WIDTH), lambda i: (i, 1)),
                  pl.BlockSpec((1, A_WIDTH), lambda i: (0, 0)),
                  pl.BlockSpec((1, A_WIDTH), lambda i: (0, 0)),
                  pl.BlockSpec((A_GROUPS, A_BLOCK, A_BLOCK), lambda i: (0, 0, 0)),
                  pl.BlockSpec((A_GROUPS, A_BLOCK, 1), lambda i: (0, 0, 0))],
        out_specs=pl.BlockSpec((tb, A_WIDTH), lambda i: (i, 0)),
        compiler_params=_params(("parallel",)),
        name="spatial_gating",
    )(h, h, ln_g.reshape(1, A_WIDTH), ln_b.reshape(1, A_WIDTH), w_s, b_s.reshape(A_GROUPS, A_BLOCK, 1))


def _dattn_body(q_ref, k_ref, v_ref, slope_ref, lq1_ref, lk1_ref, lq2_ref, lk2_ref, sg_ref, o_ref, *,
                tq, lambda_init):
    i = pl.program_id(2)
    tk = tq
    slope = slope_ref[:, 0:1]
    q = q_ref[...]
    lane = lax.broadcasted_iota(jnp.int32, q.shape, 1)
    zero = jnp.zeros_like(q)
    q2 = jnp.concatenate([jnp.where(lane < DIFF_HEAD_DIM, q, zero),
                          jnp.where(lane >= DIFF_HEAD_DIM, q, zero)], axis=0)
    q2 = q2 * jnp.asarray(DIFF_HEAD_DIM ** -0.5, q2.dtype)
    rr = lax.broadcasted_iota(jnp.int32, (2 * tq, tk), 0)
    rr = jnp.where(rr >= tq, rr - tq, rr)
    cc = lax.broadcasted_iota(jnp.int32, (2 * tq, tk), 1)
    rel = (rr - cc).astype(_F32)

    def step(j, carry, diagonal):
        m, l, acc = carry
        start = pl.multiple_of(j * tk, tk)
        kb = k_ref[pl.ds(start, tk), :]
        vb = v_ref[pl.ds(start, tk), :]
        s = lax.dot_general(q2, kb, (((1,), (1,)), ((), ())), preferred_element_type=_F32)
        off = jnp.full((1, 1), (i - j) * tq, jnp.int32).astype(_F32)
        s = s - slope * jnp.abs(rel + off)
        if diagonal:
            s = jnp.where((cc // CHUNK) <= (rr // CHUNK), s, NEG)
        m_new = jnp.maximum(m, jnp.max(s, axis=-1, keepdims=True))
        p = jnp.exp(s - m_new)
        corr = jnp.exp(m - m_new)
        l = corr * l + jnp.sum(p, axis=-1, keepdims=True)
        acc = corr * acc + jnp.dot(p.astype(_BF16), vb, preferred_element_type=_F32)
        return m_new, l, acc

    init = (jnp.full((2 * tq, 1), NEG, _F32), jnp.zeros((2 * tq, 1), _F32),
            jnp.zeros((2 * tq, DIFF_V_DIM), _F32))
    carry = lax.fori_loop(0, i, lambda j, c: step(j, c, False), init)
    m, l, acc = step(i, carry, True)
    o = acc / l
    lam = (jnp.exp(jnp.sum(lq1_ref[...] * lk1_ref[...], axis=-1, keepdims=True))
           - jnp.exp(jnp.sum(lq2_ref[...] * lk2_ref[...], axis=-1, keepdims=True)) + lambda_init)
    o = o[:tq] - lam * o[tq:]
    o = o * lax.rsqrt(jnp.mean(o * o, axis=-1, keepdims=True) + LN_EPS) * sg_ref[...]
    o_ref[...] = (o * (1.0 - lambda_init)).astype(o_ref.dtype)


def _diff_attention(h, lq1, lk1, lq2, lk2, subln_g, lambda_init, *, batch, seq, tq=256):
    n = h.shape[0]
    nq = seq // tq
    qcol, kcol, vcol = 2 * A_GROUPS, 2 * A_GROUPS + DIFF_HEADS, 2 * A_GROUPS + 2 * DIFF_HEADS
    vec = lambda a: a.reshape(1, -1).astype(_F32)
    small = lambda w: pl.BlockSpec((1, w), lambda b, hh, i: (0, 0))
    slopes = 2.0 ** (-8.0 * jnp.arange(1, DIFF_HEADS + 1, dtype=_F32) / DIFF_HEADS)
    slopes = jnp.broadcast_to(slopes[:, None, None], (DIFF_HEADS, 1, LANES))
    return pl.pallas_call(
        functools.partial(_dattn_body, tq=tq, lambda_init=lambda_init),
        out_shape=jax.ShapeDtypeStruct((n, DIFF_HEADS * DIFF_V_DIM), _BF16),
        grid=(batch, DIFF_HEADS, nq),
        in_specs=[pl.BlockSpec((tq, LANES), lambda b, hh, i: (b * nq + i, qcol + hh)),
                  pl.BlockSpec((seq, LANES), lambda b, hh, i: (b, kcol + hh)),
                  pl.BlockSpec((seq, LANES), lambda b, hh, i: (b, vcol + hh)),
                  pl.BlockSpec((None, 1, LANES), lambda b, hh, i: (hh, 0, 0)),
                  small(DIFF_HEAD_DIM), small(DIFF_HEAD_DIM), small(DIFF_HEAD_DIM), small(DIFF_HEAD_DIM),
                  small(DIFF_V_DIM)],
        out_specs=pl.BlockSpec((tq, DIFF_V_DIM), lambda b, hh, i: (b * nq + i, hh)),
        compiler_params=_params(("parallel", "parallel", "arbitrary")),
        name="diff_attention",
    )(h, h, h, slopes, vec(lq1), vec(lk1), vec(lq2), vec(lk2), vec(subln_g))


def _pool_body(h_ref, halo_ref, x_ref, wg_ref, sc_ref, wo_ref, g_ref, b_ref, o_ref, buf_ref, *, tiles_per_seq):
    t = h_ref.shape[0]
    it = pl.program_id(0) % tiles_per_seq
    halo = halo_ref[...]
    buf_ref[0:POOL_HALO, :] = jnp.where(it == 0, jnp.zeros_like(halo), halo)
    buf_ref[POOL_HALO:POOL_HALO + t, :] = h_ref[...]
    pos = it * t + lax.broadcasted_iota(jnp.int32, (t, 1), 0)
    acc = None
    for j, w in enumerate(POOL_WINDOWS):
        cols = slice(j * POOL_GROUP_CH, (j + 1) * POOL_GROUP_CH)
        hj = buf_ref[POOL_HALO:POOL_HALO + t, cols]
        tot = hj
        for s in range(1, w):
            tot = tot + buf_ref[POOL_HALO - s:POOL_HALO - s + t, cols]
        count = jnp.minimum(pos + 1, w).astype(_F32)
        pooled = tot / count - hj
        y = jnp.dot(pooled.astype(_BF16), wg_ref[j], preferred_element_type=_F32) * sc_ref[:, cols]
        part = jnp.dot(y.astype(_BF16), wo_ref[cols, :], preferred_element_type=_F32)
        acc = part if acc is None else acc + part
    o_ref[...] = _layernorm(ALPHA * x_ref[...] + acc, g_ref[...], b_ref[...])


def _pool_mixer(h, x, w_grp, scale, w_out, g, b, *, seq, t=256):
    n, d = h.shape
    tiles_per_seq = seq // t
    ratio = t // POOL_HALO
    return pl.pallas_call(
        functools.partial(_pool_body, tiles_per_seq=tiles_per_seq),
        out_shape=jax.ShapeDtypeStruct((n, d), _F32),
        grid=(n // t,),
        in_specs=[pl.BlockSpec((t, d), lambda i: (i, 0)),
                  pl.BlockSpec((POOL_HALO, d), lambda i: (jnp.maximum(i * ratio - 1, 0), 0)),
                  pl.BlockSpec((t, d), lambda i: (i, 0)),
                  pl.BlockSpec(w_grp.shape, lambda i: (0, 0, 0)),
                  pl.BlockSpec((1, d), lambda i: (0, 0)),
                  pl.BlockSpec(w_out.shape, lambda i: (0, 0)),
                  pl.BlockSpec((1, d), lambda i: (0, 0)),
                  pl.BlockSpec((1, d), lambda i: (0, 0))],
        out_specs=pl.BlockSpec((t, d), lambda i: (i, 0)),
        scratch_shapes=[pltpu.VMEM((POOL_HALO + t, d), _F32)],
        compiler_params=_params(("parallel",)),
        name="pool_mixer",
    )(h, h, x, w_grp, scale.reshape(1, d), w_out, g, b)


def _xattn_body(x_ref, wq_ref, k_ref, v_ref, wo_ref, g_ref, b_ref, o_ref):
    x = x_ref[...]
    q = jnp.dot(x.astype(_BF16), wq_ref[...], preferred_element_type=_F32).astype(_BF16)
    heads = []
    for hd in range(XA_HEADS):
        cols = slice(hd * XA_HEAD_DIM, (hd + 1) * XA_HEAD_DIM)
        s = lax.dot_general(q[:, cols], k_ref[:, cols], (((1,), (1,)), ((), ())),
                            preferred_element_type=_F32) * (XA_HEAD_DIM ** -0.5)
        s = s - jnp.max(s, axis=-1, keepdims=True)
        p = jnp.exp(s)
        p = p / jnp.sum(p, axis=-1, keepdims=True)
        heads.append(jnp.dot(p.astype(_BF16), v_ref[:, cols], preferred_element_type=_F32))
    o = jnp.concatenate(heads, axis=-1).astype(_BF16)
    c = jnp.dot(o, wo_ref[...], preferred_element_type=_F32)
    o_ref[...] = _layernorm(ALPHA * x + c, g_ref[...], b_ref[...])


def _cross_attention(x, kmem, vmem, w_q, w_o, g, b, *, seq, n_mem, tm=256):
    n, d = x.shape
    tiles_per_seq = seq // tm
    return pl.pallas_call(
        _xattn_body,
        out_shape=jax.ShapeDtypeStruct((n, d), _F32),
        grid=(n // tm,),
        in_specs=[pl.BlockSpec((tm, d), lambda i: (i, 0)),
                  pl.BlockSpec(w_q.shape, lambda i: (0, 0)),
                  pl.BlockSpec((n_mem, XA_WIDTH), lambda i: (i // tiles_per_seq, 0)),
                  pl.BlockSpec((n_mem, XA_WIDTH), lambda i: (i // tiles_per_seq, 0)),
                  pl.BlockSpec(w_o.shape, lambda i: (0, 0)),
                  pl.BlockSpec((1, d), lambda i: (0, 0)),
                  pl.BlockSpec((1, d), lambda i: (0, 0))],
        out_specs=pl.BlockSpec((tm, d), lambda i: (i, 0)),
        compiler_params=_params(("parallel",)),
        name="cross_attention",
    )(x, w_q, kmem, vmem, w_o, g, b)


def _router_body(x_ref, wr_ref, br_ref, idx_ref, gate_ref, rank_ref, cnt_ref, carry_ref):
    t = x_ref.shape[0]

    @pl.when(pl.program_id(0) == 0)
    def _():
        carry_ref[...] = jnp.zeros_like(carry_ref)

    logits = lax.dot_general(wr_ref[...], x_ref[...], (((1,), (1,)), ((), ())),
                             precision=lax.Precision.HIGHEST, preferred_element_type=_F32) + br_ref[...]
    e_iota = lax.broadcasted_iota(jnp.int32, (N_EXPERTS, t), 0).astype(_F32)
    work = logits
    vals, idxs, hots = [], [], []
    for _ in range(TOP_K):
        m = jnp.max(work, axis=0, keepdims=True)
        idx = jnp.min(jnp.where(work == m, e_iota, float(N_EXPERTS)), axis=0, keepdims=True)
        hot = e_iota == idx
        vals.append(m)
        idxs.append(idx)
        hots.append(hot)
        work = jnp.where(hot, -jnp.inf, work)
    exps = [jnp.exp(v - vals[0]) for v in vals]
    den = exps[0] + exps[1] + exps[2] + exps[3]
    cnt = sum(jnp.where(h, 1.0, 0.0) for h in hots)
    tr = lax.broadcasted_iota(jnp.int32, (t, t), 0)
    tc = lax.broadcasted_iota(jnp.int32, (t, t), 1)
    before = jnp.where(tr < tc, 1.0, 0.0).astype(_BF16)
    prior = carry_ref[...] + jnp.dot(cnt.astype(_BF16), before, preferred_element_type=_F32)
    for k in range(TOP_K):
        idx_ref[k:k + 1, :] = idxs[k].astype(jnp.int32)
        gate_ref[k:k + 1, :] = exps[k] / den
        rank_ref[k:k + 1, :] = jnp.sum(jnp.where(hots[k], prior, 0.0), axis=0, keepdims=True).astype(jnp.int32)
    total = carry_ref[...] + jnp.sum(cnt, axis=1, keepdims=True)
    carry_ref[...] = total
    cnt_ref[...] = jnp.broadcast_to(total, cnt_ref.shape).astype(jnp.int32)


def _router(x, wr_t, br, *, t=512):
    n, d = x.shape
    tok = lambda dt: jax.ShapeDtypeStruct((TOP_K, n), dt)
    tok_spec = pl.BlockSpec((TOP_K, t), lambda i: (0, i))
    return pl.pallas_call(
        _router_body,
        out_shape=(tok(jnp.int32), tok(_F32), tok(jnp.int32),
                   jax.ShapeDtypeStruct((N_EXPERTS, LANES), jnp.int32)),
        grid=(n // t,),
        in_specs=[pl.BlockSpec((t, d), lambda i: (i, 0)),
                  pl.BlockSpec((N_EXPERTS, d), lambda i: (0, 0)),
                  pl.BlockSpec((N_EXPERTS, 1), lambda i: (0, 0))],
        out_specs=(tok_spec, tok_spec, tok_spec, pl.BlockSpec((N_EXPERTS, LANES), lambda i: (0, 0))),
        scratch_shapes=[pltpu.VMEM((N_EXPERTS, 1), _F32)],
        compiler_params=_params(("arbitrary",)),
        name="router",
    )(x, wr_t, br)


def _scatter_body(dest_ref, x_ref, zero_ref, rows_ref, sem):
    del zero_ref
    t = x_ref.shape[0]

    def row_copy(tt, d):
        return pltpu.make_async_copy(x_ref.at[pl.ds(tt, 1), :], rows_ref.at[pl.ds(d, 1), :], sem)

    def issue(tt, c):
        for k in range(TOP_K):
            row_copy(tt, dest_ref[0, k, tt]).start()
        return c

    def drain(tt, c):
        for k in range(TOP_K):
            row_copy(0, 0).wait()
        return c

    lax.fori_loop(0, t, issue, 0)
    lax.fori_loop(0, t, drain, 0)


def _scatter_rows(x, dest3, n_rows, *, t=256):
    n, d = x.shape
    zeros = jnp.zeros((n_rows, d), x.dtype)
    return pl.pallas_call(
        _scatter_body,
        out_shape=jax.ShapeDtypeStruct((n_rows, d), x.dtype),
        grid=(n // t,),
        in_specs=[pl.BlockSpec((1, TOP_K, t), lambda i: (i, 0, 0), memory_space=pltpu.SMEM),
                  pl.BlockSpec((t, d), lambda i: (i, 0)),
                  pl.BlockSpec(memory_space=pl.ANY)],
        out_specs=pl.BlockSpec(memory_space=pl.ANY),
        scratch_shapes=[pltpu.SemaphoreType.DMA],
        input_output_aliases={2: 0},
        compiler_params=_params(("arbitrary",)),
        name="moe_scatter",
    )(dest3, x, zeros)


def _combine_body(dest_ref, y_ref, gate_ref, x_ref, g_ref, b_ref, o_ref, buf_ref, sem):
    t = x_ref.shape[0]

    def row_copy(k, tt, d):
        return pltpu.make_async_copy(y_ref.at[pl.ds(d, 1), :], buf_ref.at[k, pl.ds(tt, 1), :], sem)

    def issue(tt, c):
        for k in range(TOP_K):
            row_copy(k, tt, dest_ref[0, k, tt]).start()
        return c

    def drain(tt, c):
        for k in range(TOP_K):
            row_copy(k, 0, 0).wait()
        return c

    lax.fori_loop(0, t, issue, 0)
    lax.fori_loop(0, t, drain, 0)
    gate = gate_ref[...]
    f = gate[:, 0:1] * buf_ref[0]
    for k in range(1, TOP_K):
        f = f + gate[:, k:k + 1] * buf_ref[k]
    o_ref[...] = _layernorm(ALPHA * x_ref[...] + f, g_ref[...], b_ref[...])


def _combine(y_rows, dest3, gates, x, g, b, *, t=256):
    n, d = x.shape
    return pl.pallas_call(
        _combine_body,
        out_shape=jax.ShapeDtypeStruct((n, d), _F32),
        grid=(n // t,),
        in_specs=[pl.BlockSpec((1, TOP_K, t), lambda i: (i, 0, 0), memory_space=pltpu.SMEM),
                  pl.BlockSpec(memory_space=pl.ANY),
                  pl.BlockSpec((t, TOP_K), lambda i: (i, 0)),
                  pl.BlockSpec((t, d), lambda i: (i, 0)),
                  pl.BlockSpec((1, d), lambda i: (0, 0)),
                  pl.BlockSpec((1, d), lambda i: (0, 0))],
        out_specs=pl.BlockSpec((t, d), lambda i: (i, 0)),
        scratch_shapes=[pltpu.VMEM((TOP_K, t, d), _F32), pltpu.SemaphoreType.DMA],
        compiler_params=_params(("arbitrary",)),
        name="moe_combine",
    )(dest3, y_rows, gates, x, g, b)


def _moe_body(e_ref, blk0_ref, nblk_ref, misc_ref,
              xr_ref, wg_ref, wl_ref, bg_ref, bl_ref, wd_ref, bd_ref, y_ref,
              xs_ref, acc_ref, stage_ref, wgs_ref, wls_ref, wds_ref, in_sem, out_sem):
    del e_ref
    s = pl.program_id(0)
    f = pl.program_id(1)
    blk0 = blk0_ref[s]
    nblk = nblk_ref[s]

    def in_copy(j, slot):
        return pltpu.make_async_copy(xr_ref.at[blk0 + j], stage_ref.at[slot], in_sem.at[slot])

    def out_copy(j):
        return pltpu.make_async_copy(acc_ref.at[j], y_ref.at[blk0 + j], out_sem)

    @pl.when(jnp.logical_and(f == 0, nblk > 0))
    def _():
        in_copy(0, 0).start()

        def load(j, c):
            slot = j % 2

            @pl.when(j + 1 < nblk)
            def _():
                in_copy(j + 1, 1 - slot).start()

            in_copy(j, slot).wait()
            xs_ref[j] = stage_ref[slot].astype(_BF16)
            acc_ref[j] = jnp.broadcast_to(bd_ref[...], acc_ref.shape[1:])
            return c

        lax.fori_loop(0, nblk, load, 0)

    @pl.when(nblk > 0)
    def _():
        wgs_ref[...] = wg_ref[...].astype(_BF16)
        wls_ref[...] = wl_ref[...].astype(_BF16)
        wds_ref[...] = wd_ref[...].astype(_BF16)

        def block(j, c):
            xj = xs_ref[j]
            glu = jnp.dot(xj, wgs_ref[...], preferred_element_type=_F32) + bg_ref[...]
            lin = jnp.dot(xj, wls_ref[...], preferred_element_type=_F32) + bl_ref[...]
            glu = jnp.minimum(glu, SWIGLU_LIMIT)
            lin = jnp.clip(lin, -SWIGLU_LIMIT, SWIGLU_LIMIT)
            act = glu * jax.nn.sigmoid(SWIGLU_ALPHA * glu) * (lin + 1.0)
            acc_ref[j] += jnp.dot(act.astype(_BF16), wds_ref[...], preferred_element_type=_F32)
            return c

        lax.fori_loop(0, nblk, block, 0)

    @pl.when(jnp.logical_and(f == MOE_NF - 1, nblk > 0))
    def _():
        lax.fori_loop(0, nblk, lambda j, c: (out_copy(j).start(), c)[1], 0)
        lax.fori_loop(0, nblk, lambda j, c: (out_copy(j).wait(), c)[1], 0)

    @pl.when(jnp.logical_and(s == pl.num_programs(0) - 1, f == MOE_NF - 1))
    def _():
        stage_ref[0] = jnp.zeros(stage_ref.shape[1:], stage_ref.dtype)

        def fill(bb, c):
            cp = pltpu.make_async_copy(stage_ref.at[0], y_ref.at[bb], in_sem.at[0])
            cp.start()
            cp.wait()
            return c

        lax.fori_loop(misc_ref[0], y_ref.shape[0], fill, 0)


def _moe_experts(x_rows, sched, w_gu, b_gu, w_down, b_down, layer):
    n_blocks = x_rows.shape[0] // MOE_BLOCK
    d = x_rows.shape[1]
    xr3 = x_rows.reshape(n_blocks, MOE_BLOCK, d)
    sb_e, sb_blk0, sb_nblk, misc = sched
    s_max = sb_e.shape[0]

    def f_eff(s, f, nb):
        return jnp.where(nb[s] > 0, f, MOE_NF - 1)

    wg_map = lambda s, f, e, b0, nb, mi: (layer, e[s], 0, f_eff(s, f, nb))
    wl_map = lambda s, f, e, b0, nb, mi: (layer, e[s], 0, MOE_NF + f_eff(s, f, nb))
    wd_map = lambda s, f, e, b0, nb, mi: (layer, e[s], f_eff(s, f, nb), 0)
    bd_map = lambda s, f, e, b0, nb, mi: (layer, e[s], 0, 0)
    grid_spec = pltpu.PrefetchScalarGridSpec(
        num_scalar_prefetch=4,
        grid=(s_max, MOE_NF),
        in_specs=[pl.BlockSpec(memory_space=pl.ANY),
                  pl.BlockSpec((None, None, d, MOE_TF), wg_map),
                  pl.BlockSpec((None, None, d, MOE_TF), wl_map),
                  pl.BlockSpec((None, None, 1, MOE_TF), wg_map),
                  pl.BlockSpec((None, None, 1, MOE_TF), wl_map),
                  pl.BlockSpec((None, None, MOE_TF, d), wd_map),
                  pl.BlockSpec((None, None, 1, d), bd_map)],
        out_specs=pl.BlockSpec(memory_space=pl.ANY),
        scratch_shapes=[pltpu.VMEM((MOE_SB_BLOCKS, MOE_BLOCK, d), _BF16),
                        pltpu.VMEM((MOE_SB_BLOCKS, MOE_BLOCK, d), _F32),
                        pltpu.VMEM((2, MOE_BLOCK, d), _F32),
                        pltpu.VMEM((d, MOE_TF), _BF16),
                        pltpu.VMEM((d, MOE_TF), _BF16),
                        pltpu.VMEM((MOE_TF, d), _BF16),
                        pltpu.SemaphoreType.DMA((2,)),
                        pltpu.SemaphoreType.DMA],
    )
    b_gu4 = b_gu.reshape(DEPTH, N_EXPERTS, 1, 2 * D_FF)
    b_down4 = b_down.reshape(DEPTH, N_EXPERTS, 1, d)
    y3 = pl.pallas_call(
        _moe_body,
        out_shape=jax.ShapeDtypeStruct((n_blocks, MOE_BLOCK, d), _F32),
        grid_spec=grid_spec,
        compiler_params=_params(("arbitrary", "arbitrary")),
        name="moe_experts",
    )(sb_e, sb_blk0, sb_nblk, misc, xr3, w_gu, w_gu, b_gu4, b_gu4, w_down, b_down4)
    return y3.reshape(n_blocks * MOE_BLOCK, d)


def _moe_schedule(counts, s_max):
    nblk_e = (counts + MOE_BLOCK - 1) // MOE_BLOCK
    blk_start = jnp.cumsum(nblk_e) - nblk_e
    ns_e = (nblk_e + MOE_SB_BLOCKS - 1) // MOE_SB_BLOCKS
    cum_ns = jnp.cumsum(ns_e)
    total_s = cum_ns[-1]
    s = jnp.arange(s_max, dtype=jnp.int32)
    valid = s < total_s
    e_of = lambda v: jnp.minimum(jnp.searchsorted(cum_ns, v, side="right"), N_EXPERTS - 1).astype(jnp.int32)
    e = jnp.where(valid, e_of(s), e_of(jnp.maximum(total_s - 1, 0)))
    within = s - (cum_ns[e] - ns_e[e])
    blk0 = jnp.where(valid, blk_start[e] + within * MOE_SB_BLOCKS, 0)
    nblk = jnp.where(valid, jnp.minimum(MOE_SB_BLOCKS, nblk_e[e] - within * MOE_SB_BLOCKS), 0)
    misc = jnp.sum(nblk_e).reshape(1)
    i32 = lambda a: a.astype(jnp.int32)
    return (i32(e), i32(blk0), i32(nblk), i32(misc)), i32(blk_start * MOE_BLOCK)


def _moe_layer(x, w_router, b_router, w_gu, b_gu, w_down, b_down, g, b, layer, *, t=256):
    n, d = x.shape
    n_assign = n * TOP_K
    n_rows = -(-n_assign // MOE_BLOCK) * MOE_BLOCK + N_EXPERTS * MOE_BLOCK
    n_blocks = n_rows // MOE_BLOCK
    s_max = (n_blocks + N_EXPERTS * (MOE_SB_BLOCKS - 1)) // MOE_SB_BLOCKS + 1
    idx, gates, rank, cnt = _router(x, w_router.T, b_router.reshape(N_EXPERTS, 1))
    sched, row_start = _moe_schedule(cnt[:, 0], s_max)
    dest = row_start[idx] + rank
    dest3 = dest.reshape(TOP_K, n // t, t).transpose(1, 0, 2)
    x_rows = _scatter_rows(x, dest3, n_rows, t=t)
    y_rows = _moe_experts(x_rows, sched, w_gu, b_gu, w_down, b_down, layer)
    return _combine(y_rows, dest3, gates.T, x, g, b, t=t)


def kernel(x, mem, even_w_in, even_ln_v_g, even_ln_v_b, even_w_s, even_b_s, even_lam_q1, even_lam_k1, even_lam_q2, even_lam_k2, even_subln_g, even_w_out, odd_w_in, odd_w_grp, odd_scale, odd_w_out, xa_w_q, xa_w_k, xa_w_v, xa_w_o, moe_w_router, moe_b_router, moe_w_gu, moe_b_gu, moe_w_down, moe_b_down, ln_g, ln_b):
    batch, seq, d = x.shape
    n_mem = mem.shape[1]
    xt = x.reshape(batch * seq, d)
    memt = mem.reshape(batch * n_mem, d)
    bf = lambda a: a.astype(_BF16)
    ln = lambda l, j: (ln_g[l, j].reshape(1, d), ln_b[l, j].reshape(1, d))
    for l in range(DEPTH):
        i = l // 2
        if l % 2 == 0:
            lambda_init = 0.8 - 0.6 * math.exp(-0.3 * l)
            h = _mm(xt, bf(even_w_in[i]), tm=1024, tn=512, out_dtype=_BF16, gelu_cols=2 * A_WIDTH,
                    name="even_in_proj")
            a_out = _spatial_gating(h, even_ln_v_g[i], even_ln_v_b[i], even_w_s[i], even_b_s[i])
            b_out = _diff_attention(h, even_lam_q1[i], even_lam_k1[i], even_lam_q2[i], even_lam_k2[i],
                                    even_subln_g[i], lambda_init, batch=batch, seq=seq)
            xt = _mm_ln([a_out, b_out], bf(even_w_out[i]), xt, *ln(l, 0), tm=256, name="even_out_proj")
        else:
            h = _mm(xt, bf(odd_w_in[i]), tm=1024, tn=512, out_dtype=_F32, name="odd_in_proj")
            xt = _pool_mixer(h, xt, bf(odd_w_grp[i]), odd_scale[i], bf(odd_w_out[i]), *ln(l, 0), seq=seq)
        kmem = _mm(memt, bf(xa_w_k[l]), tm=memt.shape[0], tn=XA_WIDTH, out_dtype=_BF16, name="xattn_k_proj")
        vmem = _mm(memt, bf(xa_w_v[l]), tm=memt.shape[0], tn=XA_WIDTH, out_dtype=_BF16, name="xattn_v_proj")
        xt = _cross_attention(xt, kmem, vmem, bf(xa_w_q[l]), bf(xa_w_o[l]), *ln(l, 1), seq=seq, n_mem=n_mem)
        xt = _moe_layer(xt, moe_w_router[l], moe_b_router[l], moe_w_gu, moe_b_gu, moe_w_down, moe_b_down,
                        *ln(l, 2), l)
    return xt.reshape(batch, seq, d)
```

```python
import functools
import math

import jax
import jax.numpy as jnp
from jax import lax
from jax.experimental import pallas as pl
from jax.experimental.pallas import tpu as pltpu

_F32 = jnp.float32
_BF16 = jnp.bfloat16

D_MODEL = 2048
DEPTH = 2
CHUNK = 64
ALPHA = (2.0 * DEPTH) ** 0.25
LN_EPS = 1e-5
NEG = -1e30
A_WIDTH = 1024
A_BLOCK = 128
A_GROUPS = 8
DIFF_HEAD_DIM = 64
DIFF_V_DIM = 128
DIFF_HEADS = 8
POOL_WINDOWS = (2, 4, 8, 16)
POOL_GROUP_CH = 512
POOL_HALO = 16
XA_HEADS = 4
XA_HEAD_DIM = 128
XA_WIDTH = 512
N_EXPERTS = 32
TOP_K = 4
D_FF = 2048
SWIGLU_LIMIT = 7.0
SWIGLU_ALPHA = 1.702
MOE_BLOCK = 256

LANES = 128
SUBLANES = 8
VMEM_LIMIT_BYTES = 56 * 1024 * 1024

MOE_SB_BLOCKS = 6
MOE_TF = 256
MOE_NF = D_FF // MOE_TF
MOE_DOWN_COLS = 512
MOE_CHUNK_BLOCKS = 4


def _params(sem, vmem=VMEM_LIMIT_BYTES):
    return pltpu.CompilerParams(dimension_semantics=sem, vmem_limit_bytes=vmem)


def _layernorm(y, g, b):
    mu = jnp.mean(y, axis=-1, keepdims=True)
    d = y - mu
    var = jnp.mean(d * d, axis=-1, keepdims=True)
    return d * lax.rsqrt(var + LN_EPS) * g + b


def _gelu(x):
    return 0.5 * x * (1.0 + lax.erf(x * (2.0 ** -0.5)))


def _mm_body(x_ref, w_ref, o_ref, xb_ref, *, gelu_tiles):
    j = pl.program_id(1)

    @pl.when(j == 0)
    def _():
        xb_ref[...] = x_ref[...].astype(_BF16)

    acc = jnp.dot(xb_ref[...], w_ref[...], preferred_element_type=_F32)
    if gelu_tiles:
        @pl.when(j < gelu_tiles)
        def _():
            o_ref[...] = _gelu(acc).astype(o_ref.dtype)

        @pl.when(j >= gelu_tiles)
        def _():
            o_ref[...] = acc.astype(o_ref.dtype)
    else:
        o_ref[...] = acc.astype(o_ref.dtype)


def _mm(x, w, *, tm, tn, out_dtype, gelu_cols=0, name):
    m, k = x.shape
    n = w.shape[1]
    assert m % tm == 0 and n % tn == 0 and gelu_cols % tn == 0
    return pl.pallas_call(
        functools.partial(_mm_body, gelu_tiles=gelu_cols // tn),
        out_shape=jax.ShapeDtypeStruct((m, n), out_dtype),
        grid=(m // tm, n // tn),
        in_specs=[pl.BlockSpec((tm, k), lambda i, j: (i, 0)),
                  pl.BlockSpec((k, tn), lambda i, j: (0, j))],
        out_specs=pl.BlockSpec((tm, tn), lambda i, j: (i, j)),
        scratch_shapes=[pltpu.VMEM((tm, k), _BF16)],
        compiler_params=_params(("parallel", "arbitrary")),
        name=name,
    )(x, w)


def _mm_ln_body(*refs, n_a):
    a_refs = refs[:n_a]
    w_ref, r_ref, g_ref, b_ref, o_ref = refs[n_a:]
    acc = None
    row = 0
    for a_ref in a_refs:
        kk = a_ref.shape[1]
        part = jnp.dot(a_ref[...], w_ref[row:row + kk, :], preferred_element_type=_F32)
        acc = part if acc is None else acc + part
        row += kk
    o_ref[...] = _layernorm(ALPHA * r_ref[...] + acc, g_ref[...], b_ref[...])


def _mm_ln(a_list, w, resid, g, b, *, tm, name):
    m = resid.shape[0]
    n = w.shape[1]
    in_specs = [pl.BlockSpec((tm, a.shape[1]), lambda i: (i, 0)) for a in a_list]
    in_specs += [pl.BlockSpec(w.shape, lambda i: (0, 0)),
                 pl.BlockSpec((tm, n), lambda i: (i, 0)),
                 pl.BlockSpec((1, n), lambda i: (0, 0)),
                 pl.BlockSpec((1, n), lambda i: (0, 0))]
    return pl.pallas_call(
        functools.partial(_mm_ln_body, n_a=len(a_list)),
        out_shape=jax.ShapeDtypeStruct((m, n), _F32),
        grid=(m // tm,),
        in_specs=in_specs,
        out_specs=pl.BlockSpec((tm, n), lambda i: (i, 0)),
        compiler_params=_params(("parallel",)),
        name=name,
    )(*a_list, w, resid, g, b)


def _sgu_body(u_ref, gv_ref, lng_ref, lnb_ref, ws_ref, bs_ref, o_ref):
    tb = u_ref.shape[0]
    v = _layernorm(gv_ref[...].astype(_F32), lng_ref[...], lnb_ref[...]).astype(_BF16)
    r = lax.broadcasted_iota(jnp.int32, (A_BLOCK, A_BLOCK), 0) // CHUNK
    c = lax.broadcasted_iota(jnp.int32, (A_BLOCK, A_BLOCK), 1) // CHUNK
    causal = c <= r
    for g in range(A_GROUPS):
        wm = jnp.where(causal, ws_ref[g], 0.0).astype(_BF16)
        cols = slice(g * A_BLOCK, (g + 1) * A_BLOCK)
        for n in range(tb // A_BLOCK):
            rows = slice(n * A_BLOCK, (n + 1) * A_BLOCK)
            z = jnp.dot(wm, v[rows, cols], preferred_element_type=_F32) + bs_ref[g]
            o_ref[rows, cols] = (u_ref[rows, cols].astype(_F32) * z).astype(o_ref.dtype)


def _spatial_gating(h, ln_g, ln_b, w_s, b_s, *, tb=512):
    n = h.shape[0]
    return pl.pallas_call(
        _sgu_body,
        out_shape=jax.ShapeDtypeStruct((n, A_WIDTH), _BF16),
        grid=(n // tb,),
        in_specs=[pl.BlockSpec((tb, A_WIDTH), lambda i: (i, 0)),
                  pl.BlockSpec((tb, A_WIDTH), lambda i: (i, 1)),
                  pl.BlockSpec((1, A_WIDTH), lambda i: (0, 0)),
                  pl.BlockSpec((1, A_WIDTH), lambda i: (0, 0)),
                  pl.BlockSpec((A_GROUPS, A_BLOCK, A_BLOCK), lambda i: (0, 0, 0)),
                  pl.BlockSpec((A_GROUPS, A_BLOCK, 1), lambda i: (0, 0, 0))],
        out_specs=pl.BlockSpec((tb, A_WIDTH), lambda i: (i, 0)),
        compiler_params=_params(("parallel",)),
        name="spatial_gating",
    )(h, h, ln_g.reshape(1, A_WIDTH), ln_b.reshape(1, A_WIDTH), w_s, b_s.reshape(A_GROUPS, A_BLOCK, 1))


def _dattn_body(q_ref, k_ref, v_ref, slope_ref, lq1_ref, lk1_ref, lq2_ref, lk2_ref, sg_ref, o_ref,
                *scratch, tq, tk, hpg, lambda_init):
    m_ref, acc_ref, bias_ref, s0_ref, s1_ref, x0_ref, x1_ref = (
        scratch[k * hpg:(k + 1) * hpg] for k in range(7))
    s_ref, mx_ref = (s0_ref, s1_ref), (x0_ref, x1_ref)
    i = pl.program_id(2)
    rr = lax.broadcasted_iota(jnp.int32, (tq, tk), 0)
    cc = lax.broadcasted_iota(jnp.int32, (tq, tk), 1)
    rel = (rr - cc).astype(_F32)
    ones = jnp.ones((tk, LANES), _BF16)
    n_kv = ((i + 1) * tq + tk - 1) // tk
    last = n_kv - 1
    off_last = i * tq - last * tk
    dist_last = jnp.abs(rel + jnp.full((1, 1), off_last, jnp.int32).astype(_F32))
    allowed = (cc // CHUNK) <= (rr // CHUNK) + off_last // CHUNK
    lane = lax.broadcasted_iota(jnp.int32, (tq, LANES), 1)
    heads = range(hpg)
    slopes, q2s = [], []
    for a in heads:
        slope = slope_ref[a, :, 0:1]
        q = q_ref[:, a * LANES:(a + 1) * LANES]
        zero = jnp.zeros_like(q)
        q2 = jnp.concatenate([jnp.where(lane < DIFF_HEAD_DIM, q, zero),
                              jnp.where(lane >= DIFF_HEAD_DIM, q, zero)], axis=0)
        q2s.append(q2 * jnp.asarray(DIFF_HEAD_DIM ** -0.5, q2.dtype))
        slopes.append(slope)
        bias_ref[a][0] = -slope * rel
        bias_ref[a][1] = jnp.where(allowed, -slope * dist_last, NEG)
        m_ref[a][...] = jnp.full(m_ref[a].shape, NEG, _F32)
        acc_ref[a][...] = jnp.zeros(acc_ref[a].shape, _F32)

    def put_scores(a, j, slot):
        start = pl.multiple_of(j * tk, tk)
        kb = k_ref[pl.ds(start, tk), a * LANES:(a + 1) * LANES]
        bias = bias_ref[a][(j == last).astype(jnp.int32)]
        s = lax.dot_general(q2s[a], kb, (((1,), (1,)), ((), ())), preferred_element_type=_F32)
        s = s + jnp.concatenate([bias, bias], axis=0)
        s_ref[slot][a][...] = s
        mx_ref[slot][a][...] = jnp.max(s, axis=-1, keepdims=True)

    def online_softmax(a, j, slot):
        off = jnp.where(j == last, 0, i * tq - j * tk)
        shift = -slopes[a] * jnp.full((1, 1), off, jnp.int32).astype(_F32)
        m = m_ref[a][...]
        m_new = jnp.maximum(m, mx_ref[slot][a][...] + shift)
        p = jnp.exp(s_ref[slot][a][...] - (m_new - shift))
        corr = jnp.exp(m - m_new)
        vb = v_ref[pl.ds(pl.multiple_of(j * tk, tk), tk), a * LANES:(a + 1) * LANES]
        v1 = jnp.concatenate([vb, ones], axis=1)
        acc_ref[a][...] = corr * acc_ref[a][...] + jnp.dot(p.astype(_BF16), v1, preferred_element_type=_F32)
        m_ref[a][...] = m_new

    def pipelined_block(j, slot, next_slot):
        for a in heads:
            online_softmax(a, j, slot)
            put_scores(a, j + 1, next_slot)

    def block_pair(t, c):
        pipelined_block(2 * t, 0, 1)
        pipelined_block(2 * t + 1, 1, 0)
        return c

    for a in heads:
        put_scores(a, 0, 0)
    lax.fori_loop(0, last // 2, block_pair, 0)

    @pl.when(last % 2 == 1)
    def _():
        pipelined_block(last - 1, 0, 0)

    lam = (jnp.exp(jnp.sum(lq1_ref[...] * lk1_ref[...], axis=-1, keepdims=True))
           - jnp.exp(jnp.sum(lq2_ref[...] * lk2_ref[...], axis=-1, keepdims=True)) + lambda_init)
    for a in heads:
        online_softmax(a, last, 0)
        acc = acc_ref[a][...]
        o = acc[:, :DIFF_V_DIM] / acc[:, DIFF_V_DIM:DIFF_V_DIM + 1]
        o = o[:tq] - lam * o[tq:]
        o = o * lax.rsqrt(jnp.mean(o * o, axis=-1, keepdims=True) + LN_EPS) * sg_ref[...]
        o_ref[:, a * DIFF_V_DIM:(a + 1) * DIFF_V_DIM] = (o * (1.0 - lambda_init)).astype(o_ref.dtype)


def _diff_attention(h, lq1, lk1, lq2, lk2, subln_g, lambda_init, *, batch, seq, tq=256, tk=512, hpg=2):
    n = h.shape[0]
    nq = seq // tq
    w = hpg * LANES
    qcol, kcol, vcol = (2 * A_GROUPS // hpg, (2 * A_GROUPS + DIFF_HEADS) // hpg,
                        (2 * A_GROUPS + 2 * DIFF_HEADS) // hpg)
    vec = lambda a: a.reshape(1, -1).astype(_F32)
    small = lambda ww: pl.BlockSpec((1, ww), lambda b, g, i: (0, 0))
    slopes = 2.0 ** (-8.0 * jnp.arange(1, DIFF_HEADS + 1, dtype=_F32) / DIFF_HEADS)
    slopes = jnp.broadcast_to(slopes[:, None, None], (DIFF_HEADS, 1, LANES))
    return pl.pallas_call(
        functools.partial(_dattn_body, tq=tq, tk=tk, hpg=hpg, lambda_init=lambda_init),
        out_shape=jax.ShapeDtypeStruct((n, DIFF_HEADS * DIFF_V_DIM), _BF16),
        grid=(batch, DIFF_HEADS // hpg, nq),
        in_specs=[pl.BlockSpec((tq, w), lambda b, g, i: (b * nq + i, qcol + g)),
                  pl.BlockSpec((seq, w), lambda b, g, i: (b, kcol + g)),
                  pl.BlockSpec((seq, w), lambda b, g, i: (b, vcol + g)),
                  pl.BlockSpec((hpg, 1, LANES), lambda b, g, i: (g, 0, 0)),
                  small(DIFF_HEAD_DIM), small(DIFF_HEAD_DIM), small(DIFF_HEAD_DIM), small(DIFF_HEAD_DIM),
                  small(DIFF_V_DIM)],
        out_specs=pl.BlockSpec((tq, hpg * DIFF_V_DIM), lambda b, g, i: (b * nq + i, g)),
        scratch_shapes=([pltpu.VMEM((2 * tq, 1), _F32)] * hpg
                        + [pltpu.VMEM((2 * tq, 2 * DIFF_V_DIM), _F32)] * hpg
                        + [pltpu.VMEM((2, tq, tk), _F32)] * hpg
                        + [pltpu.VMEM((2 * tq, tk), _F32)] * (2 * hpg)
                        + [pltpu.VMEM((2 * tq, 1), _F32)] * (2 * hpg)),
        compiler_params=_params(("parallel", "parallel", "arbitrary")),
        name="diff_attention",
    )(h, h, h, slopes, vec(lq1), vec(lk1), vec(lq2), vec(lk2), vec(subln_g))


def _pool_body(h_ref, halo_ref, x_ref, wg_ref, sc_ref, wo_ref, g_ref, b_ref, o_ref, buf_ref, *, tiles_per_seq):
    t = h_ref.shape[0]
    it = pl.program_id(0) % tiles_per_seq
    halo = halo_ref[...]
    buf_ref[0:POOL_HALO, :] = jnp.where(it == 0, jnp.zeros_like(halo), halo)
    buf_ref[POOL_HALO:POOL_HALO + t, :] = h_ref[...]
    pos = it * t + lax.broadcasted_iota(jnp.int32, (t, 1), 0)
    acc = None
    for j, w in enumerate(POOL_WINDOWS):
        cols = slice(j * POOL_GROUP_CH, (j + 1) * POOL_GROUP_CH)
        hj = buf_ref[POOL_HALO:POOL_HALO + t, cols]
        tot = hj
        for s in range(1, w):
            tot = tot + buf_ref[POOL_HALO - s:POOL_HALO - s + t, cols]
        count = jnp.minimum(pos + 1, w).astype(_F32)
        pooled = tot / count - hj
        y = jnp.dot(pooled.astype(_BF16), wg_ref[j], preferred_element_type=_F32) * sc_ref[:, cols]
        part = jnp.dot(y.astype(_BF16), wo_ref[cols, :], preferred_element_type=_F32)
        acc = part if acc is None else acc + part
    o_ref[...] = _layernorm(ALPHA * x_ref[...] + acc, g_ref[...], b_ref[...])


def _pool_mixer(h, x, w_grp, scale, w_out, g, b, *, seq, t=256):
    n, d = h.shape
    tiles_per_seq = seq // t
    ratio = t // POOL_HALO
    return pl.pallas_call(
        functools.partial(_pool_body, tiles_per_seq=tiles_per_seq),
        out_shape=jax.ShapeDtypeStruct((n, d), _F32),
        grid=(n // t,),
        in_specs=[pl.BlockSpec((t, d), lambda i: (i, 0)),
                  pl.BlockSpec((POOL_HALO, d), lambda i: (jnp.maximum(i * ratio - 1, 0), 0)),
                  pl.BlockSpec((t, d), lambda i: (i, 0)),
                  pl.BlockSpec(w_grp.shape, lambda i: (0, 0, 0)),
                  pl.BlockSpec((1, d), lambda i: (0, 0)),
                  pl.BlockSpec(w_out.shape, lambda i: (0, 0)),
                  pl.BlockSpec((1, d), lambda i: (0, 0)),
                  pl.BlockSpec((1, d), lambda i: (0, 0))],
        out_specs=pl.BlockSpec((t, d), lambda i: (i, 0)),
        scratch_shapes=[pltpu.VMEM((POOL_HALO + t, d), _F32)],
        compiler_params=_params(("parallel",)),
        name="pool_mixer",
    )(h, h, x, w_grp, scale.reshape(1, d), w_out, g, b)


def _xattn_body(x_ref, wq_ref, k_ref, v_ref, wo_ref, g_ref, b_ref, o_ref):
    x = x_ref[...]
    q = jnp.dot(x.astype(_BF16), wq_ref[...], preferred_element_type=_F32).astype(_BF16)
    heads = []
    for hd in range(XA_HEADS):
        cols = slice(hd * XA_HEAD_DIM, (hd + 1) * XA_HEAD_DIM)
        s = lax.dot_general(q[:, cols], k_ref[:, cols], (((1,), (1,)), ((), ())),
                            preferred_element_type=_F32) * (XA_HEAD_DIM ** -0.5)
        s = s - jnp.max(s, axis=-1, keepdims=True)
        p = jnp.exp(s)
        p = p / jnp.sum(p, axis=-1, keepdims=True)
        heads.append(jnp.dot(p.astype(_BF16), v_ref[:, cols], preferred_element_type=_F32))
    o = jnp.concatenate(heads, axis=-1).astype(_BF16)
    c = jnp.dot(o, wo_ref[...], preferred_element_type=_F32)
    o_ref[...] = _layernorm(ALPHA * x + c, g_ref[...], b_ref[...])


def _cross_attention(x, kmem, vmem, w_q, w_o, g, b, *, seq, n_mem, tm=256):
    n, d = x.shape
    tiles_per_seq = seq // tm
    return pl.pallas_call(
        _xattn_body,
        out_shape=jax.ShapeDtypeStruct((n, d), _F32),
        grid=(n // tm,),
        in_specs=[pl.BlockSpec((tm, d), lambda i: (i, 0)),
                  pl.BlockSpec(w_q.shape, lambda i: (0, 0)),
                  pl.BlockSpec((n_mem, XA_WIDTH), lambda i: (i // tiles_per_seq, 0)),
                  pl.BlockSpec((n_mem, XA_WIDTH), lambda i: (i // tiles_per_seq, 0)),
                  pl.BlockSpec(w_o.shape, lambda i: (0, 0)),
                  pl.BlockSpec((1, d), lambda i: (0, 0)),
                  pl.BlockSpec((1, d), lambda i: (0, 0))],
        out_specs=pl.BlockSpec((tm, d), lambda i: (i, 0)),
        compiler_params=_params(("parallel",)),
        name="cross_attention",
    )(x, w_q, kmem, vmem, w_o, g, b)


def _router_body(x_ref, wr_ref, br_ref, idx_ref, gate_ref, rank_ref, cnt_ref, carry_ref):
    t = x_ref.shape[0]

    @pl.when(pl.program_id(0) == 0)
    def _():
        carry_ref[...] = jnp.zeros_like(carry_ref)

    logits = lax.dot_general(wr_ref[...], x_ref[...], (((1,), (1,)), ((), ())),
                             precision=lax.Precision.HIGHEST, preferred_element_type=_F32) + br_ref[...]
    e_iota = lax.broadcasted_iota(jnp.int32, (N_EXPERTS, t), 0).astype(_F32)
    work = logits
    vals, idxs, hots = [], [], []
    for _ in range(TOP_K):
        m = jnp.max(work, axis=0, keepdims=True)
        idx = jnp.min(jnp.where(work == m, e_iota, float(N_EXPERTS)), axis=0, keepdims=True)
        hot = e_iota == idx
        vals.append(m)
        idxs.append(idx)
        hots.append(hot)
        work = jnp.where(hot, -jnp.inf, work)
    exps = [jnp.exp(v - vals[0]) for v in vals]
    den = exps[0] + exps[1] + exps[2] + exps[3]
    cnt = sum(jnp.where(h, 1.0, 0.0) for h in hots)
    tr = lax.broadcasted_iota(jnp.int32, (t, t), 0)
    tc = lax.broadcasted_iota(jnp.int32, (t, t), 1)
    before = jnp.where(tr < tc, 1.0, 0.0).astype(_BF16)
    prior = carry_ref[...] + jnp.dot(cnt.astype(_BF16), before, preferred_element_type=_F32)
    for k in range(TOP_K):
        idx_ref[k:k + 1, :] = idxs[k].astype(jnp.int32)
        gate_ref[k:k + 1, :] = exps[k] / den
        rank_ref[k:k + 1, :] = jnp.sum(jnp.where(hots[k], prior, 0.0), axis=0, keepdims=True).astype(jnp.int32)
    total = carry_ref[...] + jnp.sum(cnt, axis=1, keepdims=True)
    carry_ref[...] = total
    cnt_ref[...] = jnp.broadcast_to(total, cnt_ref.shape).astype(jnp.int32)


def _router(x, wr_t, br, *, t=512):
    n, d = x.shape
    tok = lambda dt: jax.ShapeDtypeStruct((TOP_K, n), dt)
    tok_spec = pl.BlockSpec((TOP_K, t), lambda i: (0, i))
    return pl.pallas_call(
        _router_body,
        out_shape=(tok(jnp.int32), tok(_F32), tok(jnp.int32),
                   jax.ShapeDtypeStruct((N_EXPERTS, LANES), jnp.int32)),
        grid=(n // t,),
        in_specs=[pl.BlockSpec((t, d), lambda i: (i, 0)),
                  pl.BlockSpec((N_EXPERTS, d), lambda i: (0, 0)),
                  pl.BlockSpec((N_EXPERTS, 1), lambda i: (0, 0))],
        out_specs=(tok_spec, tok_spec, tok_spec, pl.BlockSpec((N_EXPERTS, LANES), lambda i: (0, 0))),
        scratch_shapes=[pltpu.VMEM((N_EXPERTS, 1), _F32)],
        compiler_params=_params(("arbitrary",)),
        name="router",
    )(x, wr_t, br)


PAD_CHUNKS = (128, 64, 32, 16, 8)


def _scatter_body(pad_start_ref, pad_len_ref, misc_ref, dest_ref, x_ref, rows_ref, zero_ref, sem, zsem):
    t = x_ref.shape[0]
    n_rows = rows_ref.shape[0]
    zrows = zero_ref.shape[0]

    @pl.when(pl.program_id(0) == 0)
    def _():
        zero_ref[...] = jnp.zeros(zero_ref.shape, zero_ref.dtype)

        def pad_copies(e, wait):
            start = pad_start_ref[e]
            length = pad_len_ref[e]
            end = start + length
            done = 0
            for chunk in PAD_CHUNKS:
                done = done + (length & chunk)
                pos = pl.multiple_of(end - done, SUBLANES)
                cp = pltpu.make_async_copy(zero_ref.at[pl.ds(0, chunk), :],
                                           rows_ref.at[pl.ds(pos, chunk), :], zsem)

                @pl.when((length & chunk) != 0)
                def _():
                    cp.wait() if wait else cp.start()

            for r in range(SUBLANES - 1):
                cp = pltpu.make_async_copy(zero_ref.at[pl.ds(0, 1), :],
                                           rows_ref.at[pl.ds(start + r, 1), :], zsem)

                @pl.when(r < (length & (SUBLANES - 1)))
                def _():
                    cp.wait() if wait else cp.start()

        def tail_copy(bb):
            return pltpu.make_async_copy(zero_ref, rows_ref.at[pl.ds(bb * zrows, zrows), :], zsem)

        tail0 = misc_ref[0] * (MOE_BLOCK // zrows)
        for wait in (False, True):
            lax.fori_loop(0, N_EXPERTS, lambda e, c: (pad_copies(e, wait), c)[1], 0)
            lax.fori_loop(tail0, n_rows // zrows,
                          lambda bb, c: ((tail_copy(bb).wait() if wait else tail_copy(bb).start()), c)[1], 0)

    def issue(tt, c):
        for k in range(TOP_K):
            pltpu.make_async_copy(x_ref.at[pl.ds(tt, 1), :],
                                  rows_ref.at[pl.ds(dest_ref[0, k, tt], 1), :], sem).start()
        return c

    lax.fori_loop(0, t, issue, 0, unroll=4)
    for k in range(TOP_K):
        pltpu.make_async_copy(x_ref, rows_ref.at[pl.ds(0, t), :], sem).wait()


def _scatter_rows(x, dest3, pad_start, pad_len, misc, n_rows, *, t=256):
    n, d = x.shape
    grid_spec = pltpu.PrefetchScalarGridSpec(
        num_scalar_prefetch=3,
        grid=(n // t,),
        in_specs=[pl.BlockSpec((1, TOP_K, t), lambda i, *_: (i, 0, 0), memory_space=pltpu.SMEM),
                  pl.BlockSpec((t, d), lambda i, *_: (i, 0))],
        out_specs=pl.BlockSpec(memory_space=pl.ANY),
        scratch_shapes=[pltpu.VMEM((PAD_CHUNKS[0], d), x.dtype),
                        pltpu.SemaphoreType.DMA, pltpu.SemaphoreType.DMA],
    )
    return pl.pallas_call(
        _scatter_body,
        out_shape=jax.ShapeDtypeStruct((n_rows, d), x.dtype),
        grid_spec=grid_spec,
        compiler_params=_params(("arbitrary",)),
        name="moe_scatter",
    )(pad_start, pad_len, misc, dest3, x)


def _combine_body(dest_ref, y_ref, gate_ref, x_ref, g_ref, b_ref, o_ref, buf_ref, sem):
    t = x_ref.shape[0]

    def row_copy(k, tt, d):
        return pltpu.make_async_copy(y_ref.at[pl.ds(d, 1), :], buf_ref.at[k, pl.ds(tt, 1), :], sem)

    def issue(tt, c):
        for k in range(TOP_K):
            row_copy(k, tt, dest_ref[0, k, tt]).start()
        return c

    lax.fori_loop(0, t, issue, 0, unroll=4)
    for k in range(TOP_K):
        pltpu.make_async_copy(y_ref.at[pl.ds(0, t), :], buf_ref.at[k], sem).wait()
    gate = gate_ref[...]
    f = gate[:, 0:1] * buf_ref[0]
    for k in range(1, TOP_K):
        f = f + gate[:, k:k + 1] * buf_ref[k]
    o_ref[...] = _layernorm(ALPHA * x_ref[...] + f, g_ref[...], b_ref[...])


def _combine(y_rows, dest3, gates, x, g, b, *, t=256):
    n, d = x.shape
    return pl.pallas_call(
        _combine_body,
        out_shape=jax.ShapeDtypeStruct((n, d), _F32),
        grid=(n // t,),
        in_specs=[pl.BlockSpec((1, TOP_K, t), lambda i: (i, 0, 0), memory_space=pltpu.SMEM),
                  pl.BlockSpec(memory_space=pl.ANY),
                  pl.BlockSpec((t, TOP_K), lambda i: (i, 0)),
                  pl.BlockSpec((t, d), lambda i: (i, 0)),
                  pl.BlockSpec((1, d), lambda i: (0, 0)),
                  pl.BlockSpec((1, d), lambda i: (0, 0))],
        out_specs=pl.BlockSpec((t, d), lambda i: (i, 0)),
        scratch_shapes=[pltpu.VMEM((TOP_K, t, d), _F32), pltpu.SemaphoreType.DMA],
        compiler_params=_params(("arbitrary",)),
        name="moe_combine",
    )(dest3, y_rows, gates, x, g, b)


def _moe_body(e_ref, blk0_ref, nblk_ref, misc_ref,
              xr_ref, wg_ref, wl_ref, bg_ref, bl_ref, wd_ref, bd_ref, y_ref,
              xs_ref, acc_ref, stage_ref, wgs_ref, wls_ref, wds_ref, in_sem, out_sem):
    del e_ref
    s = pl.program_id(0)
    f = pl.program_id(1)
    n_s = pl.num_programs(0)
    nblk = nblk_ref[s]
    d = acc_ref.shape[1]

    def block_rows(j):
        return pl.ds(pl.multiple_of(j * MOE_BLOCK, MOE_BLOCK), MOE_BLOCK)

    def in_copy(sb, j):
        return pltpu.make_async_copy(xr_ref.at[blk0_ref[sb] + j], stage_ref.at[j], in_sem.at[j])

    def out_copy(sb, j):
        return pltpu.make_async_copy(acc_ref.at[block_rows(j), :], y_ref.at[blk0_ref[sb] + j], out_sem)

    def for_blocks(n, fn):
        lax.fori_loop(0, n, lambda j, c: (fn(j), c)[1], 0)

    @pl.when(jnp.logical_and(s == 0, f == 0))
    def _():
        for_blocks(nblk, lambda j: in_copy(0, j).start())

    @pl.when(f == 0)
    def _():
        def take(j):
            in_copy(s, j).wait()
            xs_ref[block_rows(j), :] = stage_ref[j].astype(_BF16)

        for_blocks(nblk, take)

        @pl.when(s > 0)
        def _():
            prev = jnp.maximum(s - 1, 0)
            for_blocks(nblk_ref[prev], lambda j: out_copy(prev, j).wait())

        def reset(j):
            acc_ref[block_rows(j), :] = jnp.broadcast_to(bd_ref[...], (MOE_BLOCK, d))

        for_blocks(nblk, reset)

    @pl.when(jnp.logical_and(f == 1, s + 1 < n_s))
    def _():
        nxt = jnp.minimum(s + 1, n_s - 1)
        for_blocks(nblk_ref[nxt], lambda j: in_copy(nxt, j).start())

    @pl.when(nblk > 0)
    def _():
        wgs_ref[...] = wg_ref[...].astype(_BF16)
        wls_ref[...] = wl_ref[...].astype(_BF16)
        wds_ref[...] = wd_ref[...].astype(_BF16)

        def rows_chunk(row0, m):
            rows = pl.ds(pl.multiple_of(row0, MOE_BLOCK), m)
            xj = xs_ref[rows, :]
            glu = jnp.dot(xj, wgs_ref[...], preferred_element_type=_F32) + bg_ref[...]
            lin = jnp.dot(xj, wls_ref[...], preferred_element_type=_F32) + bl_ref[...]
            glu = jnp.minimum(glu, SWIGLU_LIMIT)
            lin = jnp.clip(lin, -SWIGLU_LIMIT, SWIGLU_LIMIT)
            act = (glu * jax.nn.sigmoid(SWIGLU_ALPHA * glu) * (lin + 1.0)).astype(_BF16)
            for c0 in range(0, d, MOE_DOWN_COLS):
                cols = slice(c0, c0 + MOE_DOWN_COLS)
                acc_ref[rows, cols] += jnp.dot(act, wds_ref[:, cols], preferred_element_type=_F32)

        big = MOE_CHUNK_BLOCKS * MOE_BLOCK
        lax.fori_loop(0, nblk // MOE_CHUNK_BLOCKS,
                      lambda c, carry: (rows_chunk(c * big, big), carry)[1], 0)
        part = MOE_CHUNK_BLOCKS // 2
        while part >= 1:
            @pl.when((nblk & part) != 0)
            def _(part=part):
                rows_chunk((nblk & ~(2 * part - 1)) * MOE_BLOCK, part * MOE_BLOCK)

            part //= 2

    @pl.when(f == MOE_NF - 1)
    def _():
        for_blocks(nblk, lambda j: out_copy(s, j).start())

        @pl.when(s == n_s - 1)
        def _():
            for_blocks(nblk, lambda j: out_copy(s, j).wait())
            stage_ref[0] = jnp.zeros(stage_ref.shape[1:], stage_ref.dtype)

            def fill(bb):
                cp = pltpu.make_async_copy(stage_ref.at[0], y_ref.at[bb], in_sem.at[0])
                cp.start()
                cp.wait()

            lax.fori_loop(misc_ref[0], y_ref.shape[0], lambda bb, c: (fill(bb), c)[1], 0)


def _moe_experts(x_rows, sched, w_gu, b_gu, w_down, b_down, layer):
    n_blocks = x_rows.shape[0] // MOE_BLOCK
    d = x_rows.shape[1]
    xr3 = x_rows.reshape(n_blocks, MOE_BLOCK, d)
    sb_e, sb_blk0, sb_nblk, misc = sched
    s_max = sb_e.shape[0]

    def f_eff(s, f, nb):
        return jnp.where(nb[s] > 0, f, MOE_NF - 1)

    wg_map = lambda s, f, e, b0, nb, mi: (layer, e[s], 0, f_eff(s, f, nb))
    wl_map = lambda s, f, e, b0, nb, mi: (layer, e[s], 0, MOE_NF + f_eff(s, f, nb))
    wd_map = lambda s, f, e, b0, nb, mi: (layer, e[s], f_eff(s, f, nb), 0)
    bd_map = lambda s, f, e, b0, nb, mi: (layer, e[s], 0, 0)
    grid_spec = pltpu.PrefetchScalarGridSpec(
        num_scalar_prefetch=4,
        grid=(s_max, MOE_NF),
        in_specs=[pl.BlockSpec(memory_space=pl.ANY),
                  pl.BlockSpec((None, None, d, MOE_TF), wg_map),
                  pl.BlockSpec((None, None, d, MOE_TF), wl_map),
                  pl.BlockSpec((None, None, 1, MOE_TF), wg_map),
                  pl.BlockSpec((None, None, 1, MOE_TF), wl_map),
                  pl.BlockSpec((None, None, MOE_TF, d), wd_map),
                  pl.BlockSpec((None, None, 1, d), bd_map)],
        out_specs=pl.BlockSpec(memory_space=pl.ANY),
        scratch_shapes=[pltpu.VMEM((MOE_SB_BLOCKS * MOE_BLOCK, d), _BF16),
                        pltpu.VMEM((MOE_SB_BLOCKS * MOE_BLOCK, d), _F32),
                        pltpu.VMEM((MOE_SB_BLOCKS, MOE_BLOCK, d), _F32),
                        pltpu.VMEM((d, MOE_TF), _BF16),
                        pltpu.VMEM((d, MOE_TF), _BF16),
                        pltpu.VMEM((MOE_TF, d), _BF16),
                        pltpu.SemaphoreType.DMA((MOE_SB_BLOCKS,)),
                        pltpu.SemaphoreType.DMA],
    )
    b_gu4 = b_gu.reshape(DEPTH, N_EXPERTS, 1, 2 * D_FF)
    b_down4 = b_down.reshape(DEPTH, N_EXPERTS, 1, d)
    y3 = pl.pallas_call(
        _moe_body,
        out_shape=jax.ShapeDtypeStruct((n_blocks, MOE_BLOCK, d), _F32),
        grid_spec=grid_spec,
        compiler_params=_params(("arbitrary", "arbitrary")),
        name="moe_experts",
    )(sb_e, sb_blk0, sb_nblk, misc, xr3, w_gu, w_gu, b_gu4, b_gu4, w_down, b_down4)
    return y3.reshape(n_blocks * MOE_BLOCK, d)


def _moe_schedule(counts, s_max):
    nblk_e = (counts + MOE_BLOCK - 1) // MOE_BLOCK
    blk_start = jnp.cumsum(nblk_e) - nblk_e
    ns_e = (nblk_e + MOE_SB_BLOCKS - 1) // MOE_SB_BLOCKS
    cum_ns = jnp.cumsum(ns_e)
    total_s = cum_ns[-1]
    s = jnp.arange(s_max, dtype=jnp.int32)
    valid = s < total_s
    e_of = lambda v: jnp.minimum(jnp.sum(cum_ns[None, :] <= jnp.reshape(v, (-1, 1)), axis=1),
                                 N_EXPERTS - 1).astype(jnp.int32)
    e = jnp.where(valid, e_of(s), e_of(jnp.maximum(total_s - 1, 0)))
    within = s - (cum_ns[e] - ns_e[e])
    blk0 = jnp.where(valid, blk_start[e] + within * MOE_SB_BLOCKS, 0)
    nblk = jnp.where(valid, jnp.minimum(MOE_SB_BLOCKS, nblk_e[e] - within * MOE_SB_BLOCKS), 0)
    misc = jnp.sum(nblk_e).reshape(1)
    i32 = lambda a: a.astype(jnp.int32)
    row_start = blk_start * MOE_BLOCK
    pads = (i32(row_start + counts), i32(nblk_e * MOE_BLOCK - counts))
    return (i32(e), i32(blk0), i32(nblk), i32(misc)), i32(row_start), pads


def _moe_layer(x, w_router, b_router, w_gu, b_gu, w_down, b_down, g, b, layer, *, t=256):
    n, d = x.shape
    n_assign = n * TOP_K
    n_rows = -(-n_assign // MOE_BLOCK) * MOE_BLOCK + N_EXPERTS * MOE_BLOCK
    n_blocks = n_rows // MOE_BLOCK
    s_max = (n_blocks + N_EXPERTS * (MOE_SB_BLOCKS - 1)) // MOE_SB_BLOCKS + 1
    idx, gates, rank, cnt = _router(x, w_router.T, b_router.reshape(N_EXPERTS, 1))
    sched, row_start, (pad_start, pad_len) = _moe_schedule(cnt[:, 0], s_max)
    hot = idx[None] == jnp.arange(N_EXPERTS, dtype=jnp.int32)[:, None, None]
    dest = jnp.sum(jnp.where(hot, row_start[:, None, None], 0), axis=0) + rank
    dest3 = dest.reshape(TOP_K, n // t, t).transpose(1, 0, 2)
    x_rows = _scatter_rows(x, dest3, pad_start, pad_len, sched[3], n_rows, t=t)
    y_rows = _moe_experts(x_rows, sched, w_gu, b_gu, w_down, b_down, layer)
    return _combine(y_rows, dest3, gates.T, x, g, b, t=t)


def kernel(x, mem, even_w_in, even_ln_v_g, even_ln_v_b, even_w_s, even_b_s, even_lam_q1, even_lam_k1, even_lam_q2, even_lam_k2, even_subln_g, even_w_out, odd_w_in, odd_w_grp, odd_scale, odd_w_out, xa_w_q, xa_w_k, xa_w_v, xa_w_o, moe_w_router, moe_b_router, moe_w_gu, moe_b_gu, moe_w_down, moe_b_down, ln_g, ln_b):
    batch, seq, d = x.shape
    n_mem = mem.shape[1]
    xt = x.reshape(batch * seq, d)
    memt = mem.reshape(batch * n_mem, d)
    bf = lambda a: a.astype(_BF16)
    ln = lambda l, j: (ln_g[l, j].reshape(1, d), ln_b[l, j].reshape(1, d))
    for l in range(DEPTH):
        i = l // 2
        if l % 2 == 0:
            lambda_init = 0.8 - 0.6 * math.exp(-0.3 * l)
            h = _mm(xt, bf(even_w_in[i]), tm=1024, tn=512, out_dtype=_BF16, gelu_cols=2 * A_WIDTH,
                    name="even_in_proj")
            a_out = _spatial_gating(h, even_ln_v_g[i], even_ln_v_b[i], even_w_s[i], even_b_s[i])
            b_out = _diff_attention(h, even_lam_q1[i], even_lam_k1[i], even_lam_q2[i], even_lam_k2[i],
                                    even_subln_g[i], lambda_init, batch=batch, seq=seq)
            xt = _mm_ln([a_out, b_out], bf(even_w_out[i]), xt, *ln(l, 0), tm=256, name="even_out_proj")
        else:
            h = _mm(xt, bf(odd_w_in[i]), tm=1024, tn=512, out_dtype=_F32, name="odd_in_proj")
            xt = _pool_mixer(h, xt, bf(odd_w_grp[i]), odd_scale[i], bf(odd_w_out[i]), *ln(l, 0), seq=seq)
        kmem = _mm(memt, bf(xa_w_k[l]), tm=memt.shape[0], tn=XA_WIDTH, out_dtype=_BF16, name="xattn_k_proj")
        vmem = _mm(memt, bf(xa_w_v[l]), tm=memt.shape[0], tn=XA_WIDTH, out_dtype=_BF16, name="xattn_v_proj")
        xt = _cross_attention(xt, kmem, vmem, bf(xa_w_q[l]), bf(xa_w_o[l]), *ln(l, 1), seq=seq, n_mem=n_mem)
        xt = _moe_layer(xt, moe_w_router[l], moe_b_router[l], moe_w_gu, moe_b_gu, moe_w_down, moe_b_down,
                        *ln(l, 2), l)
    return xt.reshape(batch, seq, d)
```

```python
import functools
import math

import jax
import jax.numpy as jnp
from jax import lax
from jax.experimental import pallas as pl
from jax.experimental.pallas import tpu as pltpu

_F32 = jnp.float32
_BF16 = jnp.bfloat16

D_MODEL = 2048
DEPTH = 2
CHUNK = 64
ALPHA = (2.0 * DEPTH) ** 0.25
LN_EPS = 1e-5
NEG = -1e30
A_WIDTH = 1024
A_BLOCK = 128
A_GROUPS = 8
DIFF_HEAD_DIM = 64
DIFF_V_DIM = 128
DIFF_HEADS = 8
POOL_WINDOWS = (2, 4, 8, 16)
POOL_GROUP_CH = 512
POOL_HALO = 16
XA_HEADS = 4
XA_HEAD_DIM = 128
XA_WIDTH = 512
N_EXPERTS = 32
TOP_K = 4
D_FF = 2048
SWIGLU_LIMIT = 7.0
SWIGLU_ALPHA = 1.702
MOE_BLOCK = 256

LANES = 128
SUBLANES = 8
VMEM_LIMIT_BYTES = 56 * 1024 * 1024

MOE_SB_BLOCKS = 6
MOE_TF = 256
MOE_NF = D_FF // MOE_TF
MOE_DOWN_COLS = 512
MOE_CHUNK_BLOCKS = 4


def _params(sem, vmem=VMEM_LIMIT_BYTES):
    return pltpu.CompilerParams(dimension_semantics=sem, vmem_limit_bytes=vmem)


def _layernorm(y, g, b):
    mu = jnp.mean(y, axis=-1, keepdims=True)
    d = y - mu
    var = jnp.mean(d * d, axis=-1, keepdims=True)
    return d * lax.rsqrt(var + LN_EPS) * g + b


def _gelu(x):
    return 0.5 * x * (1.0 + lax.erf(x * (2.0 ** -0.5)))


def _mm_body(x_ref, w_ref, o_ref, xb_ref, *, gelu_tiles):
    j = pl.program_id(1)

    @pl.when(j == 0)
    def _():
        xb_ref[...] = x_ref[...].astype(_BF16)

    acc = jnp.dot(xb_ref[...], w_ref[...], preferred_element_type=_F32)
    if gelu_tiles:
        @pl.when(j < gelu_tiles)
        def _():
            o_ref[...] = _gelu(acc).astype(o_ref.dtype)

        @pl.when(j >= gelu_tiles)
        def _():
            o_ref[...] = acc.astype(o_ref.dtype)
    else:
        o_ref[...] = acc.astype(o_ref.dtype)


def _mm(x, w, *, tm, tn, out_dtype, gelu_cols=0, name):
    m, k = x.shape
    n = w.shape[1]
    assert m % tm == 0 and n % tn == 0 and gelu_cols % tn == 0
    return pl.pallas_call(
        functools.partial(_mm_body, gelu_tiles=gelu_cols // tn),
        out_shape=jax.ShapeDtypeStruct((m, n), out_dtype),
        grid=(m // tm, n // tn),
        in_specs=[pl.BlockSpec((tm, k), lambda i, j: (i, 0)),
                  pl.BlockSpec((k, tn), lambda i, j: (0, j))],
        out_specs=pl.BlockSpec((tm, tn), lambda i, j: (i, j)),
        scratch_shapes=[pltpu.VMEM((tm, k), _BF16)],
        compiler_params=_params(("parallel", "arbitrary")),
        name=name,
    )(x, w)


def _mm_ln_body(*refs, n_a):
    a_refs = refs[:n_a]
    w_ref, r_ref, g_ref, b_ref, o_ref = refs[n_a:]
    acc = None
    row = 0
    for a_ref in a_refs:
        kk = a_ref.shape[1]
        part = jnp.dot(a_ref[...], w_ref[row:row + kk, :], preferred_element_type=_F32)
        acc = part if acc is None else acc + part
        row += kk
    o_ref[...] = _layernorm(ALPHA * r_ref[...] + acc, g_ref[...], b_ref[...])


def _mm_ln(a_list, w, resid, g, b, *, tm, name):
    m = resid.shape[0]
    n = w.shape[1]
    in_specs = [pl.BlockSpec((tm, a.shape[1]), lambda i: (i, 0)) for a in a_list]
    in_specs += [pl.BlockSpec(w.shape, lambda i: (0, 0)),
                 pl.BlockSpec((tm, n), lambda i: (i, 0)),
                 pl.BlockSpec((1, n), lambda i: (0, 0)),
                 pl.BlockSpec((1, n), lambda i: (0, 0))]
    return pl.pallas_call(
        functools.partial(_mm_ln_body, n_a=len(a_list)),
        out_shape=jax.ShapeDtypeStruct((m, n), _F32),
        grid=(m // tm,),
        in_specs=in_specs,
        out_specs=pl.BlockSpec((tm, n), lambda i: (i, 0)),
        compiler_params=_params(("parallel",)),
        name=name,
    )(*a_list, w, resid, g, b)


def _sgu_body(u_ref, gv_ref, lng_ref, lnb_ref, ws_ref, bs_ref, o_ref):
    tb = u_ref.shape[0]
    v = _layernorm(gv_ref[...].astype(_F32), lng_ref[...], lnb_ref[...]).astype(_BF16)
    r = lax.broadcasted_iota(jnp.int32, (A_BLOCK, A_BLOCK), 0) // CHUNK
    c = lax.broadcasted_iota(jnp.int32, (A_BLOCK, A_BLOCK), 1) // CHUNK
    causal = c <= r
    for g in range(A_GROUPS):
        wm = jnp.where(causal, ws_ref[g], 0.0).astype(_BF16)
        cols = slice(g * A_BLOCK, (g + 1) * A_BLOCK)
        for n in range(tb // A_BLOCK):
            rows = slice(n * A_BLOCK, (n + 1) * A_BLOCK)
            z = jnp.dot(wm, v[rows, cols], preferred_element_type=_F32) + bs_ref[g]
            o_ref[rows, cols] = (u_ref[rows, cols].astype(_F32) * z).astype(o_ref.dtype)


def _spatial_gating(h, ln_g, ln_b, w_s, b_s, *, tb=512):
    n = h.shape[0]
    return pl.pallas_call(
        _sgu_body,
        out_shape=jax.ShapeDtypeStruct((n, A_WIDTH), _BF16),
        grid=(n // tb,),
        in_specs=[pl.BlockSpec((tb, A_WIDTH), lambda i: (i, 0)),
                  pl.BlockSpec((tb, A_WIDTH), lambda i: (i, 1)),
                  pl.BlockSpec((1, A_WIDTH), lambda i: (0, 0)),
                  pl.BlockSpec((1, A_WIDTH), lambda i: (0, 0)),
                  pl.BlockSpec((A_GROUPS, A_BLOCK, A_BLOCK), lambda i: (0, 0, 0)),
                  pl.BlockSpec((A_GROUPS, A_BLOCK, 1), lambda i: (0, 0, 0))],
        out_specs=pl.BlockSpec((tb, A_WIDTH), lambda i: (i, 0)),
        compiler_params=_params(("parallel",)),
        name="spatial_gating",
    )(h, h, ln_g.reshape(1, A_WIDTH), ln_b.reshape(1, A_WIDTH), w_s, b_s.reshape(A_GROUPS, A_BLOCK, 1))


def _dattn_body(q_ref, k_ref, v_ref, slope_ref, lq1_ref, lk1_ref, lq2_ref, lk2_ref, sg_ref, o_ref,
                *scratch, tq, tk, hpg, lambda_init):
    m_ref, acc_ref, bias_ref, s0_ref, s1_ref, x0_ref, x1_ref = (
        scratch[k * hpg:(k + 1) * hpg] for k in range(7))
    s_ref, mx_ref = (s0_ref, s1_ref), (x0_ref, x1_ref)
    i = pl.program_id(2)
    rr = lax.broadcasted_iota(jnp.int32, (tq, tk), 0)
    cc = lax.broadcasted_iota(jnp.int32, (tq, tk), 1)
    rel = (rr - cc).astype(_F32)
    ones = jnp.ones((tk, LANES), _BF16)
    n_kv = ((i + 1) * tq + tk - 1) // tk
    last = n_kv - 1
    off_last = i * tq - last * tk
    dist_last = jnp.abs(rel + jnp.full((1, 1), off_last, jnp.int32).astype(_F32))
    allowed = (cc // CHUNK) <= (rr // CHUNK) + off_last // CHUNK
    lane = lax.broadcasted_iota(jnp.int32, (tq, LANES), 1)
    heads = range(hpg)
    slopes, q2s = [], []
    for a in heads:
        slope = slope_ref[a, :, 0:1]
        q = q_ref[:, a * LANES:(a + 1) * LANES]
        zero = jnp.zeros_like(q)
        q2 = jnp.concatenate([jnp.where(lane < DIFF_HEAD_DIM, q, zero),
                              jnp.where(lane >= DIFF_HEAD_DIM, q, zero)], axis=0)
        q2s.append(q2 * jnp.asarray(DIFF_HEAD_DIM ** -0.5, q2.dtype))
        slopes.append(slope)
        bias_ref[a][0] = -slope * rel
        bias_ref[a][1] = jnp.where(allowed, -slope * dist_last, NEG)
        m_ref[a][...] = jnp.full(m_ref[a].shape, NEG, _F32)
        acc_ref[a][...] = jnp.zeros(acc_ref[a].shape, _F32)

    def put_scores(a, j, slot):
        start = pl.multiple_of(j * tk, tk)
        kb = k_ref[pl.ds(start, tk), a * LANES:(a + 1) * LANES]
        bias = bias_ref[a][(j == last).astype(jnp.int32)]
        s = lax.dot_general(q2s[a], kb, (((1,), (1,)), ((), ())), preferred_element_type=_F32)
        s = s + jnp.concatenate([bias, bias], axis=0)
        s_ref[slot][a][...] = s
        mx_ref[slot][a][...] = jnp.max(s, axis=-1, keepdims=True)

    def online_softmax(a, j, slot):
        off = jnp.where(j == last, 0, i * tq - j * tk)
        shift = -slopes[a] * jnp.full((1, 1), off, jnp.int32).astype(_F32)
        m = m_ref[a][...]
        m_new = jnp.maximum(m, mx_ref[slot][a][...] + shift)
        p = jnp.exp(s_ref[slot][a][...] - (m_new - shift))
        corr = jnp.exp(m - m_new)
        vb = v_ref[pl.ds(pl.multiple_of(j * tk, tk), tk), a * LANES:(a + 1) * LANES]
        v1 = jnp.concatenate([vb, ones], axis=1)
        acc_ref[a][...] = corr * acc_ref[a][...] + jnp.dot(p.astype(_BF16), v1, preferred_element_type=_F32)
        m_ref[a][...] = m_new

    def pipelined_block(j, slot, next_slot):
        for a in heads:
            online_softmax(a, j, slot)
            put_scores(a, j + 1, next_slot)

    def block_pair(t, c):
        pipelined_block(2 * t, 0, 1)
        pipelined_block(2 * t + 1, 1, 0)
        return c

    for a in heads:
        put_scores(a, 0, 0)
    lax.fori_loop(0, last // 2, block_pair, 0)

    @pl.when(last % 2 == 1)
    def _():
        pipelined_block(last - 1, 0, 0)

    lam = (jnp.exp(jnp.sum(lq1_ref[...] * lk1_ref[...], axis=-1, keepdims=True))
           - jnp.exp(jnp.sum(lq2_ref[...] * lk2_ref[...], axis=-1, keepdims=True)) + lambda_init)
    for a in heads:
        online_softmax(a, last, 0)
        acc = acc_ref[a][...]
        o = acc[:, :DIFF_V_DIM] / acc[:, DIFF_V_DIM:DIFF_V_DIM + 1]
        o = o[:tq] - lam * o[tq:]
        o = o * lax.rsqrt(jnp.mean(o * o, axis=-1, keepdims=True) + LN_EPS) * sg_ref[...]
        o_ref[:, a * DIFF_V_DIM:(a + 1) * DIFF_V_DIM] = (o * (1.0 - lambda_init)).astype(o_ref.dtype)


def _diff_attention(h, lq1, lk1, lq2, lk2, subln_g, lambda_init, *, batch, seq, tq=256, tk=512, hpg=2):
    n = h.shape[0]
    nq = seq // tq
    w = hpg * LANES
    qcol, kcol, vcol = (2 * A_GROUPS // hpg, (2 * A_GROUPS + DIFF_HEADS) // hpg,
                        (2 * A_GROUPS + 2 * DIFF_HEADS) // hpg)
    vec = lambda a: a.reshape(1, -1).astype(_F32)
    small = lambda ww: pl.BlockSpec((1, ww), lambda b, g, i: (0, 0))
    slopes = 2.0 ** (-8.0 * jnp.arange(1, DIFF_HEADS + 1, dtype=_F32) / DIFF_HEADS)
    slopes = jnp.broadcast_to(slopes[:, None, None], (DIFF_HEADS, 1, LANES))
    return pl.pallas_call(
        functools.partial(_dattn_body, tq=tq, tk=tk, hpg=hpg, lambda_init=lambda_init),
        out_shape=jax.ShapeDtypeStruct((n, DIFF_HEADS * DIFF_V_DIM), _BF16),
        grid=(batch, DIFF_HEADS // hpg, nq),
        in_specs=[pl.BlockSpec((tq, w), lambda b, g, i: (b * nq + i, qcol + g)),
                  pl.BlockSpec((seq, w), lambda b, g, i: (b, kcol + g)),
                  pl.BlockSpec((seq, w), lambda b, g, i: (b, vcol + g)),
                  pl.BlockSpec((hpg, 1, LANES), lambda b, g, i: (g, 0, 0)),
                  small(DIFF_HEAD_DIM), small(DIFF_HEAD_DIM), small(DIFF_HEAD_DIM), small(DIFF_HEAD_DIM),
                  small(DIFF_V_DIM)],
        out_specs=pl.BlockSpec((tq, hpg * DIFF_V_DIM), lambda b, g, i: (b * nq + i, g)),
        scratch_shapes=([pltpu.VMEM((2 * tq, 1), _F32)] * hpg
                        + [pltpu.VMEM((2 * tq, 2 * DIFF_V_DIM), _F32)] * hpg
                        + [pltpu.VMEM((2, tq, tk), _F32)] * hpg
                        + [pltpu.VMEM((2 * tq, tk), _F32)] * (2 * hpg)
                        + [pltpu.VMEM((2 * tq, 1), _F32)] * (2 * hpg)),
        compiler_params=_params(("parallel", "parallel", "arbitrary")),
        name="diff_attention",
    )(h, h, h, slopes, vec(lq1), vec(lk1), vec(lq2), vec(lk2), vec(subln_g))


def _pool_body(h_ref, halo_ref, x_ref, wg_ref, sc_ref, wo_ref, g_ref, b_ref, o_ref, buf_ref, *, tiles_per_seq):
    t = h_ref.shape[0]
    it = pl.program_id(0) % tiles_per_seq
    halo = halo_ref[...]
    buf_ref[0:POOL_HALO, :] = jnp.where(it == 0, jnp.zeros_like(halo), halo)
    buf_ref[POOL_HALO:POOL_HALO + t, :] = h_ref[...]
    pos = it * t + lax.broadcasted_iota(jnp.int32, (t, 1), 0)
    acc = None
    for j, w in enumerate(POOL_WINDOWS):
        cols = slice(j * POOL_GROUP_CH, (j + 1) * POOL_GROUP_CH)
        hj = buf_ref[POOL_HALO:POOL_HALO + t, cols]
        tot = hj
        for s in range(1, w):
            tot = tot + buf_ref[POOL_HALO - s:POOL_HALO - s + t, cols]
        count = jnp.minimum(pos + 1, w).astype(_F32)
        pooled = tot / count - hj
        y = jnp.dot(pooled.astype(_BF16), wg_ref[j], preferred_element_type=_F32) * sc_ref[:, cols]
        part = jnp.dot(y.astype(_BF16), wo_ref[cols, :], preferred_element_type=_F32)
        acc = part if acc is None else acc + part
    o_ref[...] = _layernorm(ALPHA * x_ref[...] + acc, g_ref[...], b_ref[...])


def _pool_mixer(h, x, w_grp, scale, w_out, g, b, *, seq, t=256):
    n, d = h.shape
    tiles_per_seq = seq // t
    ratio = t // POOL_HALO
    return pl.pallas_call(
        functools.partial(_pool_body, tiles_per_seq=tiles_per_seq),
        out_shape=jax.ShapeDtypeStruct((n, d), _F32),
        grid=(n // t,),
        in_specs=[pl.BlockSpec((t, d), lambda i: (i, 0)),
                  pl.BlockSpec((POOL_HALO, d), lambda i: (jnp.maximum(i * ratio - 1, 0), 0)),
                  pl.BlockSpec((t, d), lambda i: (i, 0)),
                  pl.BlockSpec(w_grp.shape, lambda i: (0, 0, 0)),
                  pl.BlockSpec((1, d), lambda i: (0, 0)),
                  pl.BlockSpec(w_out.shape, lambda i: (0, 0)),
                  pl.BlockSpec((1, d), lambda i: (0, 0)),
                  pl.BlockSpec((1, d), lambda i: (0, 0))],
        out_specs=pl.BlockSpec((t, d), lambda i: (i, 0)),
        scratch_shapes=[pltpu.VMEM((POOL_HALO + t, d), _F32)],
        compiler_params=_params(("parallel",)),
        name="pool_mixer",
    )(h, h, x, w_grp, scale.reshape(1, d), w_out, g, b)


def _xattn_body(x_ref, wq_ref, k_ref, v_ref, wo_ref, g_ref, b_ref, o_ref):
    x = x_ref[...]
    q = jnp.dot(x.astype(_BF16), wq_ref[...], preferred_element_type=_F32).astype(_BF16)
    heads = []
    for hd in range(XA_HEADS):
        cols = slice(hd * XA_HEAD_DIM, (hd + 1) * XA_HEAD_DIM)
        s = lax.dot_general(q[:, cols], k_ref[:, cols], (((1,), (1,)), ((), ())),
                            preferred_element_type=_F32) * (XA_HEAD_DIM ** -0.5)
        s = s - jnp.max(s, axis=-1, keepdims=True)
        p = jnp.exp(s)
        p = p / jnp.sum(p, axis=-1, keepdims=True)
        heads.append(jnp.dot(p.astype(_BF16), v_ref[:, cols], preferred_element_type=_F32))
    o = jnp.concatenate(heads, axis=-1).astype(_BF16)
    c = jnp.dot(o, wo_ref[...], preferred_element_type=_F32)
    o_ref[...] = _layernorm(ALPHA * x + c, g_ref[...], b_ref[...])


def _cross_attention(x, kmem, vmem, w_q, w_o, g, b, *, seq, n_mem, tm=512):
    n, d = x.shape
    tiles_per_seq = seq // tm
    return pl.pallas_call(
        _xattn_body,
        out_shape=jax.ShapeDtypeStruct((n, d), _F32),
        grid=(n // tm,),
        in_specs=[pl.BlockSpec((tm, d), lambda i: (i, 0)),
                  pl.BlockSpec(w_q.shape, lambda i: (0, 0)),
                  pl.BlockSpec((n_mem, XA_WIDTH), lambda i: (i // tiles_per_seq, 0)),
                  pl.BlockSpec((n_mem, XA_WIDTH), lambda i: (i // tiles_per_seq, 0)),
                  pl.BlockSpec(w_o.shape, lambda i: (0, 0)),
                  pl.BlockSpec((1, d), lambda i: (0, 0)),
                  pl.BlockSpec((1, d), lambda i: (0, 0))],
        out_specs=pl.BlockSpec((tm, d), lambda i: (i, 0)),
        compiler_params=_params(("parallel",)),
        name="cross_attention",
    )(x, w_q, kmem, vmem, w_o, g, b)


def _router_body(x_ref, wr_ref, br_ref, idx_ref, gate_ref, rank_ref, cnt_ref, carry_ref):
    t = x_ref.shape[0]

    @pl.when(pl.program_id(0) == 0)
    def _():
        carry_ref[...] = jnp.zeros_like(carry_ref)

    def split(a):
        hi = a.astype(_BF16)
        return hi, (a - hi.astype(_F32)).astype(_BF16)

    def nt_dot(a, b):
        return lax.dot_general(a, b, (((1,), (1,)), ((), ())), preferred_element_type=_F32)

    w_hi, w_lo = split(wr_ref[...])
    x_hi, x_lo = split(x_ref[...])
    logits = nt_dot(w_hi, x_hi) + (nt_dot(w_hi, x_lo) + nt_dot(w_lo, x_hi)) + br_ref[...]
    e_iota = lax.broadcasted_iota(jnp.int32, (N_EXPERTS, t), 0).astype(_F32)
    work = logits
    vals, idxs, hots = [], [], []
    for _ in range(TOP_K):
        m = jnp.max(work, axis=0, keepdims=True)
        idx = jnp.min(jnp.where(work == m, e_iota, float(N_EXPERTS)), axis=0, keepdims=True)
        hot = e_iota == idx
        vals.append(m)
        idxs.append(idx)
        hots.append(hot)
        work = jnp.where(hot, -jnp.inf, work)
    exps = [jnp.exp(v - vals[0]) for v in vals]
    den = exps[0] + exps[1] + exps[2] + exps[3]
    cnt = sum(jnp.where(h, 1.0, 0.0) for h in hots)
    tr = lax.broadcasted_iota(jnp.int32, (t, t), 0)
    tc = lax.broadcasted_iota(jnp.int32, (t, t), 1)
    before = jnp.where(tr < tc, 1.0, 0.0).astype(_BF16)
    prior = carry_ref[...] + jnp.dot(cnt.astype(_BF16), before, preferred_element_type=_F32)
    for k in range(TOP_K):
        idx_ref[k:k + 1, :] = idxs[k].astype(jnp.int32)
        gate_ref[k:k + 1, :] = exps[k] / den
        rank_ref[k:k + 1, :] = jnp.sum(jnp.where(hots[k], prior, 0.0), axis=0, keepdims=True).astype(jnp.int32)
    total = carry_ref[...] + jnp.sum(cnt, axis=1, keepdims=True)
    carry_ref[...] = total
    cnt_ref[...] = jnp.broadcast_to(total, cnt_ref.shape).astype(jnp.int32)


def _router(x, wr_t, br, *, t=512):
    n, d = x.shape
    tok = lambda dt: jax.ShapeDtypeStruct((TOP_K, n), dt)
    tok_spec = pl.BlockSpec((TOP_K, t), lambda i: (0, i))
    return pl.pallas_call(
        _router_body,
        out_shape=(tok(jnp.int32), tok(_F32), tok(jnp.int32),
                   jax.ShapeDtypeStruct((N_EXPERTS, LANES), jnp.int32)),
        grid=(n // t,),
        in_specs=[pl.BlockSpec((t, d), lambda i: (i, 0)),
                  pl.BlockSpec((N_EXPERTS, d), lambda i: (0, 0)),
                  pl.BlockSpec((N_EXPERTS, 1), lambda i: (0, 0))],
        out_specs=(tok_spec, tok_spec, tok_spec, pl.BlockSpec((N_EXPERTS, LANES), lambda i: (0, 0))),
        scratch_shapes=[pltpu.VMEM((N_EXPERTS, 1), _F32)],
        compiler_params=_params(("arbitrary",)),
        name="router",
    )(x, wr_t, br)


PAD_CHUNKS = (128, 64, 32, 16, 8)


def _scatter_body(pad_start_ref, pad_len_ref, misc_ref, dest_ref, x_ref, rows_ref, zero_ref, stage_ref, sem, zsem):
    t = x_ref.shape[0]
    n_rows = rows_ref.shape[0]
    zrows = zero_ref.shape[0]

    @pl.when(pl.program_id(0) == 0)
    def _():
        zero_ref[...] = jnp.zeros(zero_ref.shape, zero_ref.dtype)

        def pad_copies(e, wait):
            start = pad_start_ref[e]
            length = pad_len_ref[e]
            end = start + length
            done = 0
            for chunk in PAD_CHUNKS:
                done = done + (length & chunk)
                pos = pl.multiple_of(end - done, SUBLANES)
                cp = pltpu.make_async_copy(zero_ref.at[pl.ds(0, chunk), :],
                                           rows_ref.at[pl.ds(pos, chunk), :], zsem)

                @pl.when((length & chunk) != 0)
                def _():
                    cp.wait() if wait else cp.start()

            for r in range(SUBLANES - 1):
                cp = pltpu.make_async_copy(zero_ref.at[pl.ds(0, 1), :],
                                           rows_ref.at[pl.ds(start + r, 1), :], zsem)

                @pl.when(r < (length & (SUBLANES - 1)))
                def _():
                    cp.wait() if wait else cp.start()

        def tail_copy(bb):
            return pltpu.make_async_copy(zero_ref, rows_ref.at[pl.ds(bb * zrows, zrows), :], zsem)

        tail0 = misc_ref[0] * (MOE_BLOCK // zrows)
        for wait in (False, True):
            lax.fori_loop(0, N_EXPERTS, lambda e, c: (pad_copies(e, wait), c)[1], 0)
            lax.fori_loop(tail0, n_rows // zrows,
                          lambda bb, c: ((tail_copy(bb).wait() if wait else tail_copy(bb).start()), c)[1], 0)

    i = pl.program_id(0)
    slot = i % 2

    def drain(sl):
        for k in range(TOP_K):
            pltpu.make_async_copy(stage_ref.at[sl], rows_ref.at[pl.ds(0, t), :], sem.at[sl]).wait()

    @pl.when(i >= 2)
    def _():
        drain(slot)

    stage_ref[slot] = x_ref[...]

    def issue(tt, c):
        for k in range(TOP_K):
            pltpu.make_async_copy(stage_ref.at[slot, pl.ds(tt, 1), :],
                                  rows_ref.at[pl.ds(dest_ref[0, k, tt], 1), :], sem.at[slot]).start()
        return c

    lax.fori_loop(0, t, issue, 0, unroll=4)

    @pl.when(i == pl.num_programs(0) - 1)
    def _():
        drain(slot)

        @pl.when(i >= 1)
        def _():
            drain(1 - slot)


def _scatter_rows(x, dest3, pad_start, pad_len, misc, n_rows, *, t=256):
    n, d = x.shape
    grid_spec = pltpu.PrefetchScalarGridSpec(
        num_scalar_prefetch=3,
        grid=(n // t,),
        in_specs=[pl.BlockSpec((1, TOP_K, t), lambda i, *_: (i, 0, 0), memory_space=pltpu.SMEM),
                  pl.BlockSpec((t, d), lambda i, *_: (i, 0))],
        out_specs=pl.BlockSpec(memory_space=pl.ANY),
        scratch_shapes=[pltpu.VMEM((PAD_CHUNKS[0], d), x.dtype),
                        pltpu.VMEM((2, t, d), x.dtype),
                        pltpu.SemaphoreType.DMA((2,)), pltpu.SemaphoreType.DMA],
    )
    return pl.pallas_call(
        _scatter_body,
        out_shape=jax.ShapeDtypeStruct((n_rows, d), x.dtype),
        grid_spec=grid_spec,
        compiler_params=_params(("arbitrary",)),
        name="moe_scatter",
    )(pad_start, pad_len, misc, dest3, x)


def _combine_body(dest_ref, next_dest_ref, y_ref, gate_ref, x_ref, g_ref, b_ref, o_ref, buf_ref, sem):
    t = x_ref.shape[0]
    i = pl.program_id(0)
    slot = i % 2

    def gather(idx_ref, sl):
        def issue(tt, c):
            for k in range(TOP_K):
                pltpu.make_async_copy(y_ref.at[pl.ds(idx_ref[0, k, tt], 1), :],
                                      buf_ref.at[sl, k, pl.ds(tt, 1), :], sem.at[sl]).start()
            return c

        lax.fori_loop(0, t, issue, 0, unroll=4)

    @pl.when(i == 0)
    def _():
        gather(dest_ref, slot)

    @pl.when(i + 1 < pl.num_programs(0))
    def _():
        gather(next_dest_ref, 1 - slot)

    for k in range(TOP_K):
        pltpu.make_async_copy(y_ref.at[pl.ds(0, t), :], buf_ref.at[slot, k], sem.at[slot]).wait()
    gate = gate_ref[...]
    f = gate[:, 0:1] * buf_ref[slot, 0]
    for k in range(1, TOP_K):
        f = f + gate[:, k:k + 1] * buf_ref[slot, k]
    o_ref[...] = _layernorm(ALPHA * x_ref[...] + f, g_ref[...], b_ref[...])


def _combine(y_rows, dest3, gates, x, g, b, *, t=256):
    n, d = x.shape
    return pl.pallas_call(
        _combine_body,
        out_shape=jax.ShapeDtypeStruct((n, d), _F32),
        grid=(n // t,),
        in_specs=[pl.BlockSpec((1, TOP_K, t), lambda i: (i, 0, 0), memory_space=pltpu.SMEM),
                  pl.BlockSpec((1, TOP_K, t), lambda i: (jnp.minimum(i + 1, n // t - 1), 0, 0),
                               memory_space=pltpu.SMEM),
                  pl.BlockSpec(memory_space=pl.ANY),
                  pl.BlockSpec((t, TOP_K), lambda i: (i, 0)),
                  pl.BlockSpec((t, d), lambda i: (i, 0)),
                  pl.BlockSpec((1, d), lambda i: (0, 0)),
                  pl.BlockSpec((1, d), lambda i: (0, 0))],
        out_specs=pl.BlockSpec((t, d), lambda i: (i, 0)),
        scratch_shapes=[pltpu.VMEM((2, TOP_K, t, d), _F32), pltpu.SemaphoreType.DMA((2,))],
        compiler_params=_params(("arbitrary",)),
        name="moe_combine",
    )(dest3, dest3, y_rows, gates, x, g, b)


def _moe_body(e_ref, blk0_ref, nblk_ref, misc_ref,
              xr_ref, wg_ref, wl_ref, bg_ref, bl_ref, wd_ref, bd_ref, y_ref,
              xs_ref, acc_ref, stage_ref, wgs_ref, wls_ref, wds_ref, in_sem, out_sem):
    del e_ref
    s = pl.program_id(0)
    f = pl.program_id(1)
    n_s = pl.num_programs(0)
    nblk = nblk_ref[s]
    d = acc_ref.shape[1]

    def block_rows(j):
        return pl.ds(pl.multiple_of(j * MOE_BLOCK, MOE_BLOCK), MOE_BLOCK)

    def in_copy(sb, j):
        return pltpu.make_async_copy(xr_ref.at[blk0_ref[sb] + j], stage_ref.at[j], in_sem.at[j])

    def out_copy(sb, j):
        return pltpu.make_async_copy(acc_ref.at[block_rows(j), :], y_ref.at[blk0_ref[sb] + j], out_sem)

    def for_blocks(n, fn):
        lax.fori_loop(0, n, lambda j, c: (fn(j), c)[1], 0)

    @pl.when(jnp.logical_and(s == 0, f == 0))
    def _():
        for_blocks(nblk, lambda j: in_copy(0, j).start())

    @pl.when(f == 0)
    def _():
        def take(j):
            in_copy(s, j).wait()
            xs_ref[block_rows(j), :] = stage_ref[j].astype(_BF16)

        for_blocks(nblk, take)

        @pl.when(s > 0)
        def _():
            prev = jnp.maximum(s - 1, 0)
            for_blocks(nblk_ref[prev], lambda j: out_copy(prev, j).wait())

        def reset(j):
            acc_ref[block_rows(j), :] = jnp.broadcast_to(bd_ref[...], (MOE_BLOCK, d))

        for_blocks(nblk, reset)

    @pl.when(jnp.logical_and(f == 1, s + 1 < n_s))
    def _():
        nxt = jnp.minimum(s + 1, n_s - 1)
        for_blocks(nblk_ref[nxt], lambda j: in_copy(nxt, j).start())

    @pl.when(nblk > 0)
    def _():
        wgs_ref[...] = wg_ref[...].astype(_BF16)
        wls_ref[...] = wl_ref[...].astype(_BF16)
        wds_ref[...] = wd_ref[...].astype(_BF16)

        def rows_chunk(row0, m):
            rows = pl.ds(pl.multiple_of(row0, MOE_BLOCK), m)
            xj = xs_ref[rows, :]
            glu = jnp.dot(xj, wgs_ref[...], preferred_element_type=_F32) + bg_ref[...]
            lin = jnp.dot(xj, wls_ref[...], preferred_element_type=_F32) + bl_ref[...]
            glu = jnp.minimum(glu, SWIGLU_LIMIT)
            lin = jnp.clip(lin, -SWIGLU_LIMIT, SWIGLU_LIMIT)
            act = (glu * jax.nn.sigmoid(SWIGLU_ALPHA * glu) * (lin + 1.0)).astype(_BF16)
            for c0 in range(0, d, MOE_DOWN_COLS):
                cols = slice(c0, c0 + MOE_DOWN_COLS)
                acc_ref[rows, cols] += jnp.dot(act, wds_ref[:, cols], preferred_element_type=_F32)

        big = MOE_CHUNK_BLOCKS * MOE_BLOCK
        lax.fori_loop(0, nblk // MOE_CHUNK_BLOCKS,
                      lambda c, carry: (rows_chunk(c * big, big), carry)[1], 0)
        part = MOE_CHUNK_BLOCKS // 2
        while part >= 1:
            @pl.when((nblk & part) != 0)
            def _(part=part):
                rows_chunk((nblk & ~(2 * part - 1)) * MOE_BLOCK, part * MOE_BLOCK)

            part //= 2

    @pl.when(f == MOE_NF - 1)
    def _():
        for_blocks(nblk, lambda j: out_copy(s, j).start())

        @pl.when(s == n_s - 1)
        def _():
            for_blocks(nblk, lambda j: out_copy(s, j).wait())
            stage_ref[0] = jnp.zeros(stage_ref.shape[1:], stage_ref.dtype)

            def fill(bb):
                cp = pltpu.make_async_copy(stage_ref.at[0], y_ref.at[bb], in_sem.at[0])
                cp.start()
                cp.wait()

            lax.fori_loop(misc_ref[0], y_ref.shape[0], lambda bb, c: (fill(bb), c)[1], 0)


def _moe_experts(x_rows, sched, w_gu, b_gu, w_down, b_down, layer):
    n_blocks = x_rows.shape[0] // MOE_BLOCK
    d = x_rows.shape[1]
    xr3 = x_rows.reshape(n_blocks, MOE_BLOCK, d)
    sb_e, sb_blk0, sb_nblk, misc = sched
    s_max = sb_e.shape[0]

    def f_eff(s, f, nb):
        return jnp.where(nb[s] > 0, f, MOE_NF - 1)

    wg_map = lambda s, f, e, b0, nb, mi: (layer, e[s], 0, f_eff(s, f, nb))
    wl_map = lambda s, f, e, b0, nb, mi: (layer, e[s], 0, MOE_NF + f_eff(s, f, nb))
    wd_map = lambda s, f, e, b0, nb, mi: (layer, e[s], f_eff(s, f, nb), 0)
    bd_map = lambda s, f, e, b0, nb, mi: (layer, e[s], 0, 0)
    grid_spec = pltpu.PrefetchScalarGridSpec(
        num_scalar_prefetch=4,
        grid=(s_max, MOE_NF),
        in_specs=[pl.BlockSpec(memory_space=pl.ANY),
                  pl.BlockSpec((None, None, d, MOE_TF), wg_map),
                  pl.BlockSpec((None, None, d, MOE_TF), wl_map),
                  pl.BlockSpec((None, None, 1, MOE_TF), wg_map),
                  pl.BlockSpec((None, None, 1, MOE_TF), wl_map),
                  pl.BlockSpec((None, None, MOE_TF, d), wd_map),
                  pl.BlockSpec((None, None, 1, d), bd_map)],
        out_specs=pl.BlockSpec(memory_space=pl.ANY),
        scratch_shapes=[pltpu.VMEM((MOE_SB_BLOCKS * MOE_BLOCK, d), _BF16),
                        pltpu.VMEM((MOE_SB_BLOCKS * MOE_BLOCK, d), _F32),
                        pltpu.VMEM((MOE_SB_BLOCKS, MOE_BLOCK, d), _F32),
                        pltpu.VMEM((d, MOE_TF), _BF16),
                        pltpu.VMEM((d, MOE_TF), _BF16),
                        pltpu.VMEM((MOE_TF, d), _BF16),
                        pltpu.SemaphoreType.DMA((MOE_SB_BLOCKS,)),
                        pltpu.SemaphoreType.DMA],
    )
    b_gu4 = b_gu.reshape(DEPTH, N_EXPERTS, 1, 2 * D_FF)
    b_down4 = b_down.reshape(DEPTH, N_EXPERTS, 1, d)
    y3 = pl.pallas_call(
        _moe_body,
        out_shape=jax.ShapeDtypeStruct((n_blocks, MOE_BLOCK, d), _F32),
        grid_spec=grid_spec,
        compiler_params=_params(("arbitrary", "arbitrary")),
        name="moe_experts",
    )(sb_e, sb_blk0, sb_nblk, misc, xr3, w_gu, w_gu, b_gu4, b_gu4, w_down, b_down4)
    return y3.reshape(n_blocks * MOE_BLOCK, d)


def _moe_schedule(counts, s_max):
    nblk_e = (counts + MOE_BLOCK - 1) // MOE_BLOCK
    blk_start = jnp.cumsum(nblk_e) - nblk_e
    ns_e = (nblk_e + MOE_SB_BLOCKS - 1) // MOE_SB_BLOCKS
    cum_ns = jnp.cumsum(ns_e)
    total_s = cum_ns[-1]
    s = jnp.arange(s_max, dtype=jnp.int32)
    valid = s < total_s
    e_of = lambda v: jnp.minimum(jnp.sum(cum_ns[None, :] <= jnp.reshape(v, (-1, 1)), axis=1),
                                 N_EXPERTS - 1).astype(jnp.int32)
    e = jnp.where(valid, e_of(s), e_of(jnp.maximum(total_s - 1, 0)))
    within = s - (cum_ns[e] - ns_e[e])
    blk0 = jnp.where(valid, blk_start[e] + within * MOE_SB_BLOCKS, 0)
    nblk = jnp.where(valid, jnp.minimum(MOE_SB_BLOCKS, nblk_e[e] - within * MOE_SB_BLOCKS), 0)
    misc = jnp.sum(nblk_e).reshape(1)
    i32 = lambda a: a.astype(jnp.int32)
    row_start = blk_start * MOE_BLOCK
    pads = (i32(row_start + counts), i32(nblk_e * MOE_BLOCK - counts))
    return (i32(e), i32(blk0), i32(nblk), i32(misc)), i32(row_start), pads


def _moe_layer(x, w_router, b_router, w_gu, b_gu, w_down, b_down, g, b, layer, *, t=256):
    n, d = x.shape
    n_assign = n * TOP_K
    n_rows = -(-n_assign // MOE_BLOCK) * MOE_BLOCK + N_EXPERTS * MOE_BLOCK
    n_blocks = n_rows // MOE_BLOCK
    s_max = (n_blocks + N_EXPERTS * (MOE_SB_BLOCKS - 1)) // MOE_SB_BLOCKS + 1
    idx, gates, rank, cnt = _router(x, w_router.T, b_router.reshape(N_EXPERTS, 1))
    sched, row_start, (pad_start, pad_len) = _moe_schedule(cnt[:, 0], s_max)
    hot = idx[None] == jnp.arange(N_EXPERTS, dtype=jnp.int32)[:, None, None]
    dest = jnp.sum(jnp.where(hot, row_start[:, None, None], 0), axis=0) + rank
    dest3 = dest.reshape(TOP_K, n // t, t).transpose(1, 0, 2)
    x_rows = _scatter_rows(x, dest3, pad_start, pad_len, sched[3], n_rows, t=t)
    y_rows = _moe_experts(x_rows, sched, w_gu, b_gu, w_down, b_down, layer)
    return _combine(y_rows, dest3, gates.T, x, g, b, t=t)


def kernel(x, mem, even_w_in, even_ln_v_g, even_ln_v_b, even_w_s, even_b_s, even_lam_q1, even_lam_k1, even_lam_q2, even_lam_k2, even_subln_g, even_w_out, odd_w_in, odd_w_grp, odd_scale, odd_w_out, xa_w_q, xa_w_k, xa_w_v, xa_w_o, moe_w_router, moe_b_router, moe_w_gu, moe_b_gu, moe_w_down, moe_b_down, ln_g, ln_b):
    batch, seq, d = x.shape
    n_mem = mem.shape[1]
    xt = x.reshape(batch * seq, d)
    memt = mem.reshape(batch * n_mem, d)
    bf = lambda a: a.astype(_BF16)
    ln = lambda l, j: (ln_g[l, j].reshape(1, d), ln_b[l, j].reshape(1, d))
    for l in range(DEPTH):
        i = l // 2
        if l % 2 == 0:
            lambda_init = 0.8 - 0.6 * math.exp(-0.3 * l)
            h = _mm(xt, bf(even_w_in[i]), tm=1024, tn=1024, out_dtype=_BF16, gelu_cols=2 * A_WIDTH,
                    name="even_in_proj")
            a_out = _spatial_gating(h, even_ln_v_g[i], even_ln_v_b[i], even_w_s[i], even_b_s[i])
            b_out = _diff_attention(h, even_lam_q1[i], even_lam_k1[i], even_lam_q2[i], even_lam_k2[i],
                                    even_subln_g[i], lambda_init, batch=batch, seq=seq)
            xt = _mm_ln([a_out, b_out], bf(even_w_out[i]), xt, *ln(l, 0), tm=512, name="even_out_proj")
        else:
            h = _mm(xt, bf(odd_w_in[i]), tm=1024, tn=1024, out_dtype=_F32, name="odd_in_proj")
            xt = _pool_mixer(h, xt, bf(odd_w_grp[i]), odd_scale[i], bf(odd_w_out[i]), *ln(l, 0), seq=seq)
        kmem = _mm(memt, bf(xa_w_k[l]), tm=memt.shape[0], tn=XA_WIDTH, out_dtype=_BF16, name="xattn_k_proj")
        vmem = _mm(memt, bf(xa_w_v[l]), tm=memt.shape[0], tn=XA_WIDTH, out_dtype=_BF16, name="xattn_v_proj")
        xt = _cross_attention(xt, kmem, vmem, bf(xa_w_q[l]), bf(xa_w_o[l]), *ln(l, 1), seq=seq, n_mem=n_mem)
        xt = _moe_layer(xt, moe_w_router[l], moe_b_router[l], moe_w_gu, moe_b_gu, moe_w_down, moe_b_down,
                        *ln(l, 2), l)
    return xt.reshape(batch, seq, d)
```

```python
import functools
import math

import jax
import jax.numpy as jnp
from jax import lax
from jax.experimental import pallas as pl
from jax.experimental.pallas import tpu as pltpu

_F32 = jnp.float32
_BF16 = jnp.bfloat16

D_MODEL = 2048
DEPTH = 2
CHUNK = 64
ALPHA = (2.0 * DEPTH) ** 0.25
LN_EPS = 1e-5
NEG = -1e30
A_WIDTH = 1024
A_BLOCK = 128
A_GROUPS = 8
DIFF_HEAD_DIM = 64
DIFF_V_DIM = 128
DIFF_HEADS = 8
POOL_WINDOWS = (2, 4, 8, 16)
POOL_GROUP_CH = 512
POOL_HALO = 16
XA_HEADS = 4
XA_HEAD_DIM = 128
XA_WIDTH = 512
N_EXPERTS = 32
TOP_K = 4
D_FF = 2048
SWIGLU_LIMIT = 7.0
SWIGLU_ALPHA = 1.702
MOE_BLOCK = 256

LANES = 128
SUBLANES = 8
DMA_PRIORITIES = 2
VMEM_LIMIT_BYTES = 56 * 1024 * 1024
MOE_VMEM_LIMIT_BYTES = 62 * 1024 * 1024

MOE_SB_BLOCKS = 5
MOE_TF = 512
MOE_NF = D_FF // MOE_TF
MOE_DOWN_COLS = 512
MOE_CHUNK_BLOCKS = 4


def _params(sem, vmem=VMEM_LIMIT_BYTES):
    return pltpu.CompilerParams(dimension_semantics=sem, vmem_limit_bytes=vmem)


def _layernorm(y, g, b):
    mu = jnp.mean(y, axis=-1, keepdims=True)
    d = y - mu
    var = jnp.mean(d * d, axis=-1, keepdims=True)
    return d * lax.rsqrt(var + LN_EPS) * g + b


def _gelu(x):
    return 0.5 * x * (1.0 + lax.erf(x * (2.0 ** -0.5)))


def _mm_body(x_ref, w_ref, o_ref, xb_ref, *, gelu_tiles):
    j = pl.program_id(1)

    @pl.when(j == 0)
    def _():
        xb_ref[...] = x_ref[...].astype(_BF16)

    acc = jnp.dot(xb_ref[...], w_ref[...], preferred_element_type=_F32)
    if gelu_tiles:
        @pl.when(j < gelu_tiles)
        def _():
            o_ref[...] = _gelu(acc).astype(o_ref.dtype)

        @pl.when(j >= gelu_tiles)
        def _():
            o_ref[...] = acc.astype(o_ref.dtype)
    else:
        o_ref[...] = acc.astype(o_ref.dtype)


def _mm(x, w, *, tm, tn, out_dtype, gelu_cols=0, name):
    m, k = x.shape
    n = w.shape[1]
    assert m % tm == 0 and n % tn == 0 and gelu_cols % tn == 0
    return pl.pallas_call(
        functools.partial(_mm_body, gelu_tiles=gelu_cols // tn),
        out_shape=jax.ShapeDtypeStruct((m, n), out_dtype),
        grid=(m // tm, n // tn),
        in_specs=[pl.BlockSpec((tm, k), lambda i, j: (i, 0)),
                  pl.BlockSpec((k, tn), lambda i, j: (0, j))],
        out_specs=pl.BlockSpec((tm, tn), lambda i, j: (i, j)),
        scratch_shapes=[pltpu.VMEM((tm, k), _BF16)],
        compiler_params=_params(("parallel", "arbitrary")),
        name=name,
    )(x, w)


def _mm_ln_body(*refs, n_a):
    a_refs = refs[:n_a]
    w_ref, r_ref, g_ref, b_ref, o_ref = refs[n_a:]
    acc = None
    row = 0
    for a_ref in a_refs:
        kk = a_ref.shape[1]
        part = jnp.dot(a_ref[...], w_ref[row:row + kk, :], preferred_element_type=_F32)
        acc = part if acc is None else acc + part
        row += kk
    o_ref[...] = _layernorm(ALPHA * r_ref[...] + acc, g_ref[...], b_ref[...])


def _mm_ln(a_list, w, resid, g, b, *, tm, name):
    m = resid.shape[0]
    n = w.shape[1]
    in_specs = [pl.BlockSpec((tm, a.shape[1]), lambda i: (i, 0)) for a in a_list]
    in_specs += [pl.BlockSpec(w.shape, lambda i: (0, 0)),
                 pl.BlockSpec((tm, n), lambda i: (i, 0)),
                 pl.BlockSpec((1, n), lambda i: (0, 0)),
                 pl.BlockSpec((1, n), lambda i: (0, 0))]
    return pl.pallas_call(
        functools.partial(_mm_ln_body, n_a=len(a_list)),
        out_shape=jax.ShapeDtypeStruct((m, n), _F32),
        grid=(m // tm,),
        in_specs=in_specs,
        out_specs=pl.BlockSpec((tm, n), lambda i: (i, 0)),
        compiler_params=_params(("parallel",)),
        name=name,
    )(*a_list, w, resid, g, b)


def _sgu_body(u_ref, gv_ref, lng_ref, lnb_ref, ws_ref, bs_ref, o_ref):
    tb = u_ref.shape[0]
    v = _layernorm(gv_ref[...].astype(_F32), lng_ref[...], lnb_ref[...]).astype(_BF16)
    r = lax.broadcasted_iota(jnp.int32, (A_BLOCK, A_BLOCK), 0) // CHUNK
    c = lax.broadcasted_iota(jnp.int32, (A_BLOCK, A_BLOCK), 1) // CHUNK
    causal = c <= r
    for g in range(A_GROUPS):
        wm = jnp.where(causal, ws_ref[g], 0.0).astype(_BF16)
        cols = slice(g * A_BLOCK, (g + 1) * A_BLOCK)
        for n in range(tb // A_BLOCK):
            rows = slice(n * A_BLOCK, (n + 1) * A_BLOCK)
            z = jnp.dot(wm, v[rows, cols], preferred_element_type=_F32) + bs_ref[g]
            o_ref[rows, cols] = (u_ref[rows, cols].astype(_F32) * z).astype(o_ref.dtype)


def _spatial_gating(h, ln_g, ln_b, w_s, b_s, *, tb=512):
    n = h.shape[0]
    return pl.pallas_call(
        _sgu_body,
        out_shape=jax.ShapeDtypeStruct((n, A_WIDTH), _BF16),
        grid=(n // tb,),
        in_specs=[pl.BlockSpec((tb, A_WIDTH), lambda i: (i, 0)),
                  pl.BlockSpec((tb, A_WIDTH), lambda i: (i, 1)),
                  pl.BlockSpec((1, A_WIDTH), lambda i: (0, 0)),
                  pl.BlockSpec((1, A_WIDTH), lambda i: (0, 0)),
                  pl.BlockSpec((A_GROUPS, A_BLOCK, A_BLOCK), lambda i: (0, 0, 0)),
                  pl.BlockSpec((A_GROUPS, A_BLOCK, 1), lambda i: (0, 0, 0))],
        out_specs=pl.BlockSpec((tb, A_WIDTH), lambda i: (i, 0)),
        compiler_params=_params(("parallel",)),
        name="spatial_gating",
    )(h, h, ln_g.reshape(1, A_WIDTH), ln_b.reshape(1, A_WIDTH), w_s, b_s.reshape(A_GROUPS, A_BLOCK, 1))


def _dattn_body(q_ref, k_ref, v_ref, slope_ref, lq1_ref, lk1_ref, lq2_ref, lk2_ref, sg_ref, o_ref,
                *scratch, tq, tk, hpg, lambda_init):
    m_ref, acc_ref, bias_ref, s0_ref, s1_ref, x0_ref, x1_ref = (
        scratch[k * hpg:(k + 1) * hpg] for k in range(7))
    s_ref, mx_ref = (s0_ref, s1_ref), (x0_ref, x1_ref)
    i = pl.program_id(2)
    rr = lax.broadcasted_iota(jnp.int32, (tq, tk), 0)
    cc = lax.broadcasted_iota(jnp.int32, (tq, tk), 1)
    rel = (rr - cc).astype(_F32)
    ones = jnp.ones((tk, LANES), _BF16)
    n_kv = ((i + 1) * tq + tk - 1) // tk
    last = n_kv - 1
    off_last = i * tq - last * tk
    dist_last = jnp.abs(rel + jnp.full((1, 1), off_last, jnp.int32).astype(_F32))
    allowed = (cc // CHUNK) <= (rr // CHUNK) + off_last // CHUNK
    lane = lax.broadcasted_iota(jnp.int32, (tq, LANES), 1)
    heads = range(hpg)
    slopes, q2s = [], []
    for a in heads:
        slope = slope_ref[a, :, 0:1]
        q = q_ref[:, a * LANES:(a + 1) * LANES]
        zero = jnp.zeros_like(q)
        q2 = jnp.concatenate([jnp.where(lane < DIFF_HEAD_DIM, q, zero),
                              jnp.where(lane >= DIFF_HEAD_DIM, q, zero)], axis=0)
        q2s.append(q2 * jnp.asarray(DIFF_HEAD_DIM ** -0.5, q2.dtype))
        slopes.append(slope)
        bias_ref[a][0] = -slope * rel
        bias_ref[a][1] = jnp.where(allowed, -slope * dist_last, NEG)
        m_ref[a][...] = jnp.full(m_ref[a].shape, NEG, _F32)
        acc_ref[a][...] = jnp.zeros(acc_ref[a].shape, _F32)

    def put_scores(a, j, slot):
        start = pl.multiple_of(j * tk, tk)
        kb = k_ref[pl.ds(start, tk), a * LANES:(a + 1) * LANES]
        bias = bias_ref[a][(j == last).astype(jnp.int32)]
        s = lax.dot_general(q2s[a], kb, (((1,), (1,)), ((), ())), preferred_element_type=_F32)
        s = s + jnp.concatenate([bias, bias], axis=0)
        s_ref[slot][a][...] = s
        mx_ref[slot][a][...] = jnp.max(s, axis=-1, keepdims=True)

    def online_softmax(a, j, slot):
        off = jnp.where(j == last, 0, i * tq - j * tk)
        shift = -slopes[a] * jnp.full((1, 1), off, jnp.int32).astype(_F32)
        m = m_ref[a][...]
        m_new = jnp.maximum(m, mx_ref[slot][a][...] + shift)
        p = jnp.exp(s_ref[slot][a][...] - (m_new - shift))
        corr = jnp.exp(m - m_new)
        vb = v_ref[pl.ds(pl.multiple_of(j * tk, tk), tk), a * LANES:(a + 1) * LANES]
        v1 = jnp.concatenate([vb, ones], axis=1)
        acc_ref[a][...] = corr * acc_ref[a][...] + jnp.dot(p.astype(_BF16), v1, preferred_element_type=_F32)
        m_ref[a][...] = m_new

    def pipelined_block(j, slot, next_slot):
        for a in heads:
            online_softmax(a, j, slot)
            put_scores(a, j + 1, next_slot)

    def block_pair(t, c):
        pipelined_block(2 * t, 0, 1)
        pipelined_block(2 * t + 1, 1, 0)
        return c

    for a in heads:
        put_scores(a, 0, 0)
    lax.fori_loop(0, last // 2, block_pair, 0)

    @pl.when(last % 2 == 1)
    def _():
        pipelined_block(last - 1, 0, 0)

    lam = (jnp.exp(jnp.sum(lq1_ref[...] * lk1_ref[...], axis=-1, keepdims=True))
           - jnp.exp(jnp.sum(lq2_ref[...] * lk2_ref[...], axis=-1, keepdims=True)) + lambda_init)
    for a in heads:
        online_softmax(a, last, 0)
        acc = acc_ref[a][...]
        o = acc[:, :DIFF_V_DIM] / acc[:, DIFF_V_DIM:DIFF_V_DIM + 1]
        o = o[:tq] - lam * o[tq:]
        o = o * lax.rsqrt(jnp.mean(o * o, axis=-1, keepdims=True) + LN_EPS) * sg_ref[...]
        o_ref[:, a * DIFF_V_DIM:(a + 1) * DIFF_V_DIM] = (o * (1.0 - lambda_init)).astype(o_ref.dtype)


def _diff_attention(h, lq1, lk1, lq2, lk2, subln_g, lambda_init, *, batch, seq, tq=256, tk=512, hpg=2):
    n = h.shape[0]
    nq = seq // tq
    w = hpg * LANES
    qcol, kcol, vcol = (2 * A_GROUPS // hpg, (2 * A_GROUPS + DIFF_HEADS) // hpg,
                        (2 * A_GROUPS + 2 * DIFF_HEADS) // hpg)
    vec = lambda a: a.reshape(1, -1).astype(_F32)
    small = lambda ww: pl.BlockSpec((1, ww), lambda b, g, i: (0, 0))
    slopes = 2.0 ** (-8.0 * jnp.arange(1, DIFF_HEADS + 1, dtype=_F32) / DIFF_HEADS)
    slopes = jnp.broadcast_to(slopes[:, None, None], (DIFF_HEADS, 1, LANES))
    return pl.pallas_call(
        functools.partial(_dattn_body, tq=tq, tk=tk, hpg=hpg, lambda_init=lambda_init),
        out_shape=jax.ShapeDtypeStruct((n, DIFF_HEADS * DIFF_V_DIM), _BF16),
        grid=(batch, DIFF_HEADS // hpg, nq),
        in_specs=[pl.BlockSpec((tq, w), lambda b, g, i: (b * nq + i, qcol + g)),
                  pl.BlockSpec((seq, w), lambda b, g, i: (b, kcol + g)),
                  pl.BlockSpec((seq, w), lambda b, g, i: (b, vcol + g)),
                  pl.BlockSpec((hpg, 1, LANES), lambda b, g, i: (g, 0, 0)),
                  small(DIFF_HEAD_DIM), small(DIFF_HEAD_DIM), small(DIFF_HEAD_DIM), small(DIFF_HEAD_DIM),
                  small(DIFF_V_DIM)],
        out_specs=pl.BlockSpec((tq, hpg * DIFF_V_DIM), lambda b, g, i: (b * nq + i, g)),
        scratch_shapes=([pltpu.VMEM((2 * tq, 1), _F32)] * hpg
                        + [pltpu.VMEM((2 * tq, 2 * DIFF_V_DIM), _F32)] * hpg
                        + [pltpu.VMEM((2, tq, tk), _F32)] * hpg
                        + [pltpu.VMEM((2 * tq, tk), _F32)] * (2 * hpg)
                        + [pltpu.VMEM((2 * tq, 1), _F32)] * (2 * hpg)),
        compiler_params=_params(("parallel", "parallel", "arbitrary")),
        name="diff_attention",
    )(h, h, h, slopes, vec(lq1), vec(lk1), vec(lq2), vec(lk2), vec(subln_g))


def _pool_body(h_ref, halo_ref, x_ref, wg_ref, sc_ref, wo_ref, g_ref, b_ref, o_ref, buf_ref, *, tiles_per_seq):
    t = h_ref.shape[0]
    it = pl.program_id(0) % tiles_per_seq
    halo = halo_ref[...]
    buf_ref[0:POOL_HALO, :] = jnp.where(it == 0, jnp.zeros_like(halo), halo)
    buf_ref[POOL_HALO:POOL_HALO + t, :] = h_ref[...]
    pos = it * t + lax.broadcasted_iota(jnp.int32, (t, 1), 0)
    acc = None
    for j, w in enumerate(POOL_WINDOWS):
        cols = slice(j * POOL_GROUP_CH, (j + 1) * POOL_GROUP_CH)
        hj = buf_ref[POOL_HALO:POOL_HALO + t, cols]
        tot = hj
        for s in range(1, w):
            tot = tot + buf_ref[POOL_HALO - s:POOL_HALO - s + t, cols]
        count = jnp.minimum(pos + 1, w).astype(_F32)
        pooled = tot / count - hj
        y = jnp.dot(pooled.astype(_BF16), wg_ref[j], preferred_element_type=_F32) * sc_ref[:, cols]
        part = jnp.dot(y.astype(_BF16), wo_ref[cols, :], preferred_element_type=_F32)
        acc = part if acc is None else acc + part
    o_ref[...] = _layernorm(ALPHA * x_ref[...] + acc, g_ref[...], b_ref[...])


def _pool_mixer(h, x, w_grp, scale, w_out, g, b, *, seq, t=256):
    n, d = h.shape
    tiles_per_seq = seq // t
    ratio = t // POOL_HALO
    return pl.pallas_call(
        functools.partial(_pool_body, tiles_per_seq=tiles_per_seq),
        out_shape=jax.ShapeDtypeStruct((n, d), _F32),
        grid=(n // t,),
        in_specs=[pl.BlockSpec((t, d), lambda i: (i, 0)),
                  pl.BlockSpec((POOL_HALO, d), lambda i: (jnp.maximum(i * ratio - 1, 0), 0)),
                  pl.BlockSpec((t, d), lambda i: (i, 0)),
                  pl.BlockSpec(w_grp.shape, lambda i: (0, 0, 0)),
                  pl.BlockSpec((1, d), lambda i: (0, 0)),
                  pl.BlockSpec(w_out.shape, lambda i: (0, 0)),
                  pl.BlockSpec((1, d), lambda i: (0, 0)),
                  pl.BlockSpec((1, d), lambda i: (0, 0))],
        out_specs=pl.BlockSpec((t, d), lambda i: (i, 0)),
        scratch_shapes=[pltpu.VMEM((POOL_HALO + t, d), _F32)],
        compiler_params=_params(("parallel",)),
        name="pool_mixer",
    )(h, h, x, w_grp, scale.reshape(1, d), w_out, g, b)


def _xattn_body(x_ref, wq_ref, k_ref, v_ref, wo_ref, g_ref, b_ref, o_ref):
    x = x_ref[...]
    q = jnp.dot(x.astype(_BF16), wq_ref[...], preferred_element_type=_F32).astype(_BF16)
    heads = []
    for hd in range(XA_HEADS):
        cols = slice(hd * XA_HEAD_DIM, (hd + 1) * XA_HEAD_DIM)
        s = lax.dot_general(q[:, cols], k_ref[:, cols], (((1,), (1,)), ((), ())),
                            preferred_element_type=_F32) * (XA_HEAD_DIM ** -0.5)
        s = s - jnp.max(s, axis=-1, keepdims=True)
        p = jnp.exp(s)
        p = p / jnp.sum(p, axis=-1, keepdims=True)
        heads.append(jnp.dot(p.astype(_BF16), v_ref[:, cols], preferred_element_type=_F32))
    o = jnp.concatenate(heads, axis=-1).astype(_BF16)
    c = jnp.dot(o, wo_ref[...], preferred_element_type=_F32)
    o_ref[...] = _layernorm(ALPHA * x + c, g_ref[...], b_ref[...])


def _cross_attention(x, kmem, vmem, w_q, w_o, g, b, *, seq, n_mem, tm=512):
    n, d = x.shape
    tiles_per_seq = seq // tm
    return pl.pallas_call(
        _xattn_body,
        out_shape=jax.ShapeDtypeStruct((n, d), _F32),
        grid=(n // tm,),
        in_specs=[pl.BlockSpec((tm, d), lambda i: (i, 0)),
                  pl.BlockSpec(w_q.shape, lambda i: (0, 0)),
                  pl.BlockSpec((n_mem, XA_WIDTH), lambda i: (i // tiles_per_seq, 0)),
                  pl.BlockSpec((n_mem, XA_WIDTH), lambda i: (i // tiles_per_seq, 0)),
                  pl.BlockSpec(w_o.shape, lambda i: (0, 0)),
                  pl.BlockSpec((1, d), lambda i: (0, 0)),
                  pl.BlockSpec((1, d), lambda i: (0, 0))],
        out_specs=pl.BlockSpec((tm, d), lambda i: (i, 0)),
        compiler_params=_params(("parallel",)),
        name="cross_attention",
    )(x, w_q, kmem, vmem, w_o, g, b)


def _router_body(x_ref, wr_ref, br_ref, idx_ref, gate_ref, rank_ref, cnt_ref, carry_ref):
    t = x_ref.shape[0]

    @pl.when(pl.program_id(0) == 0)
    def _():
        carry_ref[...] = jnp.zeros_like(carry_ref)

    def split(a):
        hi = a.astype(_BF16)
        return hi, (a - hi.astype(_F32)).astype(_BF16)

    def nt_dot(a, b):
        return lax.dot_general(a, b, (((1,), (1,)), ((), ())), preferred_element_type=_F32)

    w_hi, w_lo = split(wr_ref[...])
    x_hi, x_lo = split(x_ref[...])
    logits = nt_dot(w_hi, x_hi) + (nt_dot(w_hi, x_lo) + nt_dot(w_lo, x_hi)) + br_ref[...]
    e_iota = lax.broadcasted_iota(jnp.int32, (N_EXPERTS, t), 0).astype(_F32)
    work = logits
    vals, idxs, hots = [], [], []
    for _ in range(TOP_K):
        m = jnp.max(work, axis=0, keepdims=True)
        idx = jnp.min(jnp.where(work == m, e_iota, float(N_EXPERTS)), axis=0, keepdims=True)
        hot = e_iota == idx
        vals.append(m)
        idxs.append(idx)
        hots.append(hot)
        work = jnp.where(hot, -jnp.inf, work)
    exps = [jnp.exp(v - vals[0]) for v in vals]
    den = exps[0] + exps[1] + exps[2] + exps[3]
    cnt = sum(jnp.where(h, 1.0, 0.0) for h in hots)
    tr = lax.broadcasted_iota(jnp.int32, (t, t), 0)
    tc = lax.broadcasted_iota(jnp.int32, (t, t), 1)
    before = jnp.where(tr < tc, 1.0, 0.0).astype(_BF16)
    prior = carry_ref[...] + jnp.dot(cnt.astype(_BF16), before, preferred_element_type=_F32)
    for k in range(TOP_K):
        idx_ref[k:k + 1, :] = idxs[k].astype(jnp.int32)
        gate_ref[k:k + 1, :] = exps[k] / den
        rank_ref[k:k + 1, :] = jnp.sum(jnp.where(hots[k], prior, 0.0), axis=0, keepdims=True).astype(jnp.int32)
    total = carry_ref[...] + jnp.sum(cnt, axis=1, keepdims=True)
    carry_ref[...] = total
    cnt_ref[...] = jnp.broadcast_to(total, cnt_ref.shape).astype(jnp.int32)


def _router(x, wr_t, br, *, t=512):
    n, d = x.shape
    tok = lambda dt: jax.ShapeDtypeStruct((TOP_K, n), dt)
    tok_spec = pl.BlockSpec((TOP_K, t), lambda i: (0, i))
    return pl.pallas_call(
        _router_body,
        out_shape=(tok(jnp.int32), tok(_F32), tok(jnp.int32),
                   jax.ShapeDtypeStruct((N_EXPERTS, LANES), jnp.int32)),
        grid=(n // t,),
        in_specs=[pl.BlockSpec((t, d), lambda i: (i, 0)),
                  pl.BlockSpec((N_EXPERTS, d), lambda i: (0, 0)),
                  pl.BlockSpec((N_EXPERTS, 1), lambda i: (0, 0))],
        out_specs=(tok_spec, tok_spec, tok_spec, pl.BlockSpec((N_EXPERTS, LANES), lambda i: (0, 0))),
        scratch_shapes=[pltpu.VMEM((N_EXPERTS, 1), _F32)],
        compiler_params=_params(("arbitrary",)),
        name="router",
    )(x, wr_t, br)


PAD_CHUNKS = (128, 64, 32, 16, 8)


def _scatter_body(pad_start_ref, pad_len_ref, misc_ref, dest_ref, x_ref, rows_ref, zero_ref, stage_ref, sem, zsem):
    t = x_ref.shape[0]
    n_rows = rows_ref.shape[0]
    zrows = zero_ref.shape[0]

    @pl.when(pl.program_id(0) == 0)
    def _():
        zero_ref[...] = jnp.zeros(zero_ref.shape, zero_ref.dtype)

        def pad_copies(e, wait):
            start = pad_start_ref[e]
            length = pad_len_ref[e]
            end = start + length
            done = 0
            for chunk in PAD_CHUNKS:
                done = done + (length & chunk)
                pos = pl.multiple_of(end - done, SUBLANES)
                cp = pltpu.make_async_copy(zero_ref.at[pl.ds(0, chunk), :],
                                           rows_ref.at[pl.ds(pos, chunk), :], zsem)

                @pl.when((length & chunk) != 0)
                def _():
                    cp.wait() if wait else cp.start()

            for r in range(SUBLANES - 1):
                cp = pltpu.make_async_copy(zero_ref.at[pl.ds(0, 1), :],
                                           rows_ref.at[pl.ds(start + r, 1), :], zsem)

                @pl.when(r < (length & (SUBLANES - 1)))
                def _():
                    cp.wait() if wait else cp.start()

        def tail_copy(bb):
            return pltpu.make_async_copy(zero_ref, rows_ref.at[pl.ds(bb * zrows, zrows), :], zsem)

        tail0 = misc_ref[0] * (MOE_BLOCK // zrows)
        for wait in (False, True):
            lax.fori_loop(0, N_EXPERTS, lambda e, c: (pad_copies(e, wait), c)[1], 0)
            lax.fori_loop(tail0, n_rows // zrows,
                          lambda bb, c: ((tail_copy(bb).wait() if wait else tail_copy(bb).start()), c)[1], 0)

    i = pl.program_id(0)
    slot = i % 2

    def drain(sl):
        for k in range(TOP_K):
            pltpu.make_async_copy(stage_ref.at[sl], rows_ref.at[pl.ds(0, t), :], sem.at[sl]).wait()

    @pl.when(i >= 2)
    def _():
        drain(slot)

    stage_ref[slot] = x_ref[...]

    def issue(tt, c):
        for k in range(TOP_K):
            pltpu.make_async_copy(stage_ref.at[slot, pl.ds(tt, 1), :],
                                  rows_ref.at[pl.ds(dest_ref[0, k, tt], 1), :],
                                  sem.at[slot]).start(priority=k % DMA_PRIORITIES)
        return c

    lax.fori_loop(0, t, issue, 0, unroll=4)

    @pl.when(i == pl.num_programs(0) - 1)
    def _():
        drain(slot)

        @pl.when(i >= 1)
        def _():
            drain(1 - slot)


def _scatter_rows(x, dest3, pad_start, pad_len, misc, n_rows, *, t=256):
    n, d = x.shape
    grid_spec = pltpu.PrefetchScalarGridSpec(
        num_scalar_prefetch=3,
        grid=(n // t,),
        in_specs=[pl.BlockSpec((1, TOP_K, t), lambda i, *_: (i, 0, 0), memory_space=pltpu.SMEM),
                  pl.BlockSpec((t, d), lambda i, *_: (i, 0))],
        out_specs=pl.BlockSpec(memory_space=pl.ANY),
        scratch_shapes=[pltpu.VMEM((PAD_CHUNKS[0], d), x.dtype),
                        pltpu.VMEM((2, t, d), x.dtype),
                        pltpu.SemaphoreType.DMA((2,)), pltpu.SemaphoreType.DMA],
    )
    return pl.pallas_call(
        _scatter_body,
        out_shape=jax.ShapeDtypeStruct((n_rows, d), x.dtype),
        grid_spec=grid_spec,
        compiler_params=_params(("arbitrary",)),
        name="moe_scatter",
    )(pad_start, pad_len, misc, dest3, x)


def _combine_body(dest_ref, next_dest_ref, y_ref, gate_ref, x_ref, g_ref, b_ref, o_ref, buf_ref, sem):
    t = x_ref.shape[0]
    i = pl.program_id(0)
    slot = i % 2

    def gather(idx_ref, sl):
        def issue(tt, c):
            for k in range(TOP_K):
                pltpu.make_async_copy(y_ref.at[pl.ds(idx_ref[0, k, tt], 1), :],
                                      buf_ref.at[sl, k, pl.ds(tt, 1), :],
                                      sem.at[sl]).start(priority=k % DMA_PRIORITIES)
            return c

        lax.fori_loop(0, t, issue, 0, unroll=4)

    @pl.when(i == 0)
    def _():
        gather(dest_ref, slot)

    @pl.when(i + 1 < pl.num_programs(0))
    def _():
        gather(next_dest_ref, 1 - slot)

    for k in range(TOP_K):
        pltpu.make_async_copy(y_ref.at[pl.ds(0, t), :], buf_ref.at[slot, k], sem.at[slot]).wait()
    gate = gate_ref[...]
    f = gate[:, 0:1] * buf_ref[slot, 0]
    for k in range(1, TOP_K):
        f = f + gate[:, k:k + 1] * buf_ref[slot, k]
    o_ref[...] = _layernorm(ALPHA * x_ref[...] + f, g_ref[...], b_ref[...])


def _combine(y_rows, dest3, gates, x, g, b, *, t=256):
    n, d = x.shape
    return pl.pallas_call(
        _combine_body,
        out_shape=jax.ShapeDtypeStruct((n, d), _F32),
        grid=(n // t,),
        in_specs=[pl.BlockSpec((1, TOP_K, t), lambda i: (i, 0, 0), memory_space=pltpu.SMEM),
                  pl.BlockSpec((1, TOP_K, t), lambda i: (jnp.minimum(i + 1, n // t - 1), 0, 0),
                               memory_space=pltpu.SMEM),
                  pl.BlockSpec(memory_space=pl.ANY),
                  pl.BlockSpec((t, TOP_K), lambda i: (i, 0)),
                  pl.BlockSpec((t, d), lambda i: (i, 0)),
                  pl.BlockSpec((1, d), lambda i: (0, 0)),
                  pl.BlockSpec((1, d), lambda i: (0, 0))],
        out_specs=pl.BlockSpec((t, d), lambda i: (i, 0)),
        scratch_shapes=[pltpu.VMEM((2, TOP_K, t, d), _F32), pltpu.SemaphoreType.DMA((2,))],
        compiler_params=_params(("arbitrary",)),
        name="moe_combine",
    )(dest3, dest3, y_rows, gates, x, g, b)


def _moe_body(e_ref, blk0_ref, nblk_ref, misc_ref,
              xr_ref, wg_ref, wl_ref, bg_ref, bl_ref, wd_ref, bd_ref, y_ref,
              xs_ref, acc_ref, stage_ref, wgl_ref, wds_ref, in_sem, out_sem):
    del e_ref
    s = pl.program_id(0)
    f = pl.program_id(1)
    n_s = pl.num_programs(0)
    nblk = nblk_ref[s]
    d = acc_ref.shape[1]

    def block_rows(j):
        return pl.ds(pl.multiple_of(j * MOE_BLOCK, MOE_BLOCK), MOE_BLOCK)

    def in_copy(sb, j):
        return pltpu.make_async_copy(xr_ref.at[blk0_ref[sb] + j], stage_ref.at[j], in_sem.at[j])

    def out_copy(sb, j):
        return pltpu.make_async_copy(acc_ref.at[block_rows(j), :], y_ref.at[blk0_ref[sb] + j], out_sem)

    def for_blocks(n, fn):
        lax.fori_loop(0, n, lambda j, c: (fn(j), c)[1], 0)

    @pl.when(jnp.logical_and(s == 0, f == 0))
    def _():
        for_blocks(nblk, lambda j: in_copy(0, j).start())

    @pl.when(f == 0)
    def _():
        def take(j):
            in_copy(s, j).wait()
            xs_ref[block_rows(j), :] = stage_ref[j].astype(_BF16)

        for_blocks(nblk, take)

        @pl.when(s > 0)
        def _():
            prev = jnp.maximum(s - 1, 0)
            for_blocks(nblk_ref[prev], lambda j: out_copy(prev, j).wait())

        def reset(j):
            acc_ref[block_rows(j), :] = jnp.broadcast_to(bd_ref[...], (MOE_BLOCK, d))

        for_blocks(nblk, reset)

    @pl.when(jnp.logical_and(f == 1, s + 1 < n_s))
    def _():
        nxt = jnp.minimum(s + 1, n_s - 1)
        for_blocks(nblk_ref[nxt], lambda j: in_copy(nxt, j).start())

    @pl.when(nblk > 0)
    def _():
        wgl_ref[:, :MOE_TF] = wg_ref[...].astype(_BF16)
        wgl_ref[:, MOE_TF:] = wl_ref[...].astype(_BF16)
        wds_ref[...] = wd_ref[...].astype(_BF16)

        def rows_chunk(row0, m):
            rows = pl.ds(pl.multiple_of(row0, MOE_BLOCK), m)
            h = jnp.dot(xs_ref[rows, :], wgl_ref[...], preferred_element_type=_F32)
            glu = h[:, :MOE_TF] + bg_ref[...]
            lin = h[:, MOE_TF:] + bl_ref[...]
            glu = jnp.minimum(glu, SWIGLU_LIMIT)
            lin = jnp.clip(lin, -SWIGLU_LIMIT, SWIGLU_LIMIT)
            act = (glu * jax.nn.sigmoid(SWIGLU_ALPHA * glu) * (lin + 1.0)).astype(_BF16)
            for c0 in range(0, d, MOE_DOWN_COLS):
                cols = slice(c0, c0 + MOE_DOWN_COLS)
                acc_ref[rows, cols] += jnp.dot(act, wds_ref[:, cols], preferred_element_type=_F32)

        big = MOE_CHUNK_BLOCKS * MOE_BLOCK
        lax.fori_loop(0, nblk // MOE_CHUNK_BLOCKS,
                      lambda c, carry: (rows_chunk(c * big, big), carry)[1], 0)
        part = MOE_CHUNK_BLOCKS // 2
        while part >= 1:
            @pl.when((nblk & part) != 0)
            def _(part=part):
                rows_chunk((nblk & ~(2 * part - 1)) * MOE_BLOCK, part * MOE_BLOCK)

            part //= 2

    @pl.when(f == MOE_NF - 1)
    def _():
        for_blocks(nblk, lambda j: out_copy(s, j).start())

        @pl.when(s == n_s - 1)
        def _():
            for_blocks(nblk, lambda j: out_copy(s, j).wait())
            stage_ref[0] = jnp.zeros(stage_ref.shape[1:], stage_ref.dtype)

            def fill(bb):
                cp = pltpu.make_async_copy(stage_ref.at[0], y_ref.at[bb], in_sem.at[0])
                cp.start()
                cp.wait()

            lax.fori_loop(misc_ref[0], y_ref.shape[0], lambda bb, c: (fill(bb), c)[1], 0)


def _moe_experts(x_rows, sched, w_gu, b_gu, w_down, b_down, layer):
    n_blocks = x_rows.shape[0] // MOE_BLOCK
    d = x_rows.shape[1]
    xr3 = x_rows.reshape(n_blocks, MOE_BLOCK, d)
    sb_e, sb_blk0, sb_nblk, misc = sched
    s_max = sb_e.shape[0]

    def f_eff(s, f, nb):
        return jnp.where(nb[s] > 0, f, MOE_NF - 1)

    wg_map = lambda s, f, e, b0, nb, mi: (layer, e[s], 0, f_eff(s, f, nb))
    wl_map = lambda s, f, e, b0, nb, mi: (layer, e[s], 0, MOE_NF + f_eff(s, f, nb))
    wd_map = lambda s, f, e, b0, nb, mi: (layer, e[s], f_eff(s, f, nb), 0)
    bd_map = lambda s, f, e, b0, nb, mi: (layer, e[s], 0, 0)
    grid_spec = pltpu.PrefetchScalarGridSpec(
        num_scalar_prefetch=4,
        grid=(s_max, MOE_NF),
        in_specs=[pl.BlockSpec(memory_space=pl.ANY),
                  pl.BlockSpec((None, None, d, MOE_TF), wg_map),
                  pl.BlockSpec((None, None, d, MOE_TF), wl_map),
                  pl.BlockSpec((None, None, 1, MOE_TF), wg_map),
                  pl.BlockSpec((None, None, 1, MOE_TF), wl_map),
                  pl.BlockSpec((None, None, MOE_TF, d), wd_map),
                  pl.BlockSpec((None, None, 1, d), bd_map)],
        out_specs=pl.BlockSpec(memory_space=pl.ANY),
        scratch_shapes=[pltpu.VMEM((MOE_SB_BLOCKS * MOE_BLOCK, d), _BF16),
                        pltpu.VMEM((MOE_SB_BLOCKS * MOE_BLOCK, d), _F32),
                        pltpu.VMEM((MOE_SB_BLOCKS, MOE_BLOCK, d), _F32),
                        pltpu.VMEM((d, 2 * MOE_TF), _BF16),
                        pltpu.VMEM((MOE_TF, d), _BF16),
                        pltpu.SemaphoreType.DMA((MOE_SB_BLOCKS,)),
                        pltpu.SemaphoreType.DMA],
    )
    b_gu4 = b_gu.reshape(DEPTH, N_EXPERTS, 1, 2 * D_FF)
    b_down4 = b_down.reshape(DEPTH, N_EXPERTS, 1, d)
    y3 = pl.pallas_call(
        _moe_body,
        out_shape=jax.ShapeDtypeStruct((n_blocks, MOE_BLOCK, d), _F32),
        grid_spec=grid_spec,
        compiler_params=_params(("arbitrary", "arbitrary"), vmem=MOE_VMEM_LIMIT_BYTES),
        name="moe_experts",
    )(sb_e, sb_blk0, sb_nblk, misc, xr3, w_gu, w_gu, b_gu4, b_gu4, w_down, b_down4)
    return y3.reshape(n_blocks * MOE_BLOCK, d)


def _moe_schedule(counts, s_max):
    nblk_e = (counts + MOE_BLOCK - 1) // MOE_BLOCK
    blk_start = jnp.cumsum(nblk_e) - nblk_e
    ns_e = (nblk_e + MOE_SB_BLOCKS - 1) // MOE_SB_BLOCKS
    cum_ns = jnp.cumsum(ns_e)
    total_s = cum_ns[-1]
    s = jnp.arange(s_max, dtype=jnp.int32)
    valid = s < total_s
    e_of = lambda v: jnp.minimum(jnp.sum(cum_ns[None, :] <= jnp.reshape(v, (-1, 1)), axis=1),
                                 N_EXPERTS - 1).astype(jnp.int32)
    e = jnp.where(valid, e_of(s), e_of(jnp.maximum(total_s - 1, 0)))
    within = s - (cum_ns[e] - ns_e[e])
    blk0 = jnp.where(valid, blk_start[e] + within * MOE_SB_BLOCKS, 0)
    nblk = jnp.where(valid, jnp.minimum(MOE_SB_BLOCKS, nblk_e[e] - within * MOE_SB_BLOCKS), 0)
    misc = jnp.sum(nblk_e).reshape(1)
    i32 = lambda a: a.astype(jnp.int32)
    row_start = blk_start * MOE_BLOCK
    pads = (i32(row_start + counts), i32(nblk_e * MOE_BLOCK - counts))
    return (i32(e), i32(blk0), i32(nblk), i32(misc)), i32(row_start), pads


def _moe_layer(x, w_router, b_router, w_gu, b_gu, w_down, b_down, g, b, layer, *, t=256):
    n, d = x.shape
    n_assign = n * TOP_K
    n_rows = -(-n_assign // MOE_BLOCK) * MOE_BLOCK + N_EXPERTS * MOE_BLOCK
    n_blocks = n_rows // MOE_BLOCK
    s_max = (n_blocks + N_EXPERTS * (MOE_SB_BLOCKS - 1)) // MOE_SB_BLOCKS + 1
    idx, gates, rank, cnt = _router(x, w_router.T, b_router.reshape(N_EXPERTS, 1))
    sched, row_start, (pad_start, pad_len) = _moe_schedule(cnt[:, 0], s_max)
    hot = idx[None] == jnp.arange(N_EXPERTS, dtype=jnp.int32)[:, None, None]
    dest = jnp.sum(jnp.where(hot, row_start[:, None, None], 0), axis=0) + rank
    dest3 = dest.reshape(TOP_K, n // t, t).transpose(1, 0, 2)
    x_rows = _scatter_rows(x, dest3, pad_start, pad_len, sched[3], n_rows, t=t)
    y_rows = _moe_experts(x_rows, sched, w_gu, b_gu, w_down, b_down, layer)
    return _combine(y_rows, dest3, gates.T, x, g, b, t=t)


def kernel(x, mem, even_w_in, even_ln_v_g, even_ln_v_b, even_w_s, even_b_s, even_lam_q1, even_lam_k1, even_lam_q2, even_lam_k2, even_subln_g, even_w_out, odd_w_in, odd_w_grp, odd_scale, odd_w_out, xa_w_q, xa_w_k, xa_w_v, xa_w_o, moe_w_router, moe_b_router, moe_w_gu, moe_b_gu, moe_w_down, moe_b_down, ln_g, ln_b):
    batch, seq, d = x.shape
    n_mem = mem.shape[1]
    xt = x.reshape(batch * seq, d)
    memt = mem.reshape(batch * n_mem, d)
    bf = lambda a: a.astype(_BF16)
    ln = lambda l, j: (ln_g[l, j].reshape(1, d), ln_b[l, j].reshape(1, d))
    for l in range(DEPTH):
        i = l // 2
        if l % 2 == 0:
            lambda_init = 0.8 - 0.6 * math.exp(-0.3 * l)
            h = _mm(xt, bf(even_w_in[i]), tm=1024, tn=1024, out_dtype=_BF16, gelu_cols=2 * A_WIDTH,
                    name="even_in_proj")
            a_out = _spatial_gating(h, even_ln_v_g[i], even_ln_v_b[i], even_w_s[i], even_b_s[i])
            b_out = _diff_attention(h, even_lam_q1[i], even_lam_k1[i], even_lam_q2[i], even_lam_k2[i],
                                    even_subln_g[i], lambda_init, batch=batch, seq=seq)
            xt = _mm_ln([a_out, b_out], bf(even_w_out[i]), xt, *ln(l, 0), tm=512, name="even_out_proj")
        else:
            h = _mm(xt, bf(odd_w_in[i]), tm=1024, tn=1024, out_dtype=_F32, name="odd_in_proj")
            xt = _pool_mixer(h, xt, bf(odd_w_grp[i]), odd_scale[i], bf(odd_w_out[i]), *ln(l, 0), seq=seq)
        kmem = _mm(memt, bf(xa_w_k[l]), tm=memt.shape[0], tn=XA_WIDTH, out_dtype=_BF16, name="xattn_k_proj")
        vmem = _mm(memt, bf(xa_w_v[l]), tm=memt.shape[0], tn=XA_WIDTH, out_dtype=_BF16, name="xattn_v_proj")
        xt = _cross_attention(xt, kmem, vmem, bf(xa_w_q[l]), bf(xa_w_o[l]), *ln(l, 1), seq=seq, n_mem=n_mem)
        xt = _moe_layer(xt, moe_w_router[l], moe_b_router[l], moe_w_gu, moe_b_gu, moe_w_down, moe_b_down,
                        *ln(l, 2), l)
    return xt.reshape(batch, seq, d)
```

```python
import functools
import math

import jax
import jax.numpy as jnp
from jax import lax
from jax.experimental import pallas as pl
from jax.experimental.pallas import tpu as pltpu

_F32 = jnp.float32
_BF16 = jnp.bfloat16

D_MODEL = 2048
DEPTH = 2
CHUNK = 64
ALPHA = (2.0 * DEPTH) ** 0.25
LN_EPS = 1e-5
NEG = -1e30
A_WIDTH = 1024
A_BLOCK = 128
A_GROUPS = 8
DIFF_HEAD_DIM = 64
DIFF_V_DIM = 128
DIFF_HEADS = 8
POOL_WINDOWS = (2, 4, 8, 16)
POOL_GROUP_CH = 512
POOL_HALO = 16
XA_HEADS = 4
XA_HEAD_DIM = 128
XA_WIDTH = 512
N_EXPERTS = 32
TOP_K = 4
D_FF = 2048
SWIGLU_LIMIT = 7.0
SWIGLU_ALPHA = 1.702
MOE_BLOCK = 256

LANES = 128
SUBLANES = 8
DMA_PRIORITIES = 2
VMEM_LIMIT_BYTES = 56 * 1024 * 1024
MOE_VMEM_LIMIT_BYTES = 62 * 1024 * 1024

MOE_SB_BLOCKS = 5
MOE_TF = 512
MOE_NF = D_FF // MOE_TF
MOE_DOWN_COLS = 512
MOE_CHUNK_BLOCKS = 4


def _params(sem, vmem=VMEM_LIMIT_BYTES):
    return pltpu.CompilerParams(dimension_semantics=sem, vmem_limit_bytes=vmem)


def _layernorm(y, g, b):
    mu = jnp.mean(y, axis=-1, keepdims=True)
    d = y - mu
    var = jnp.mean(d * d, axis=-1, keepdims=True)
    return d * lax.rsqrt(var + LN_EPS) * g + b


def _gelu(x):
    return 0.5 * x * (1.0 + lax.erf(x * (2.0 ** -0.5)))


def _mm_body(x_ref, w_ref, o_ref, xb_ref, *, gelu_tiles):
    j = pl.program_id(1)

    @pl.when(j == 0)
    def _():
        xb_ref[...] = x_ref[...].astype(_BF16)

    acc = jnp.dot(xb_ref[...], w_ref[...], preferred_element_type=_F32)
    if gelu_tiles:
        @pl.when(j < gelu_tiles)
        def _():
            o_ref[...] = _gelu(acc).astype(o_ref.dtype)

        @pl.when(j >= gelu_tiles)
        def _():
            o_ref[...] = acc.astype(o_ref.dtype)
    else:
        o_ref[...] = acc.astype(o_ref.dtype)


def _mm(x, w, *, tm, tn, out_dtype, gelu_cols=0, name):
    m, k = x.shape
    n = w.shape[1]
    assert m % tm == 0 and n % tn == 0 and gelu_cols % tn == 0
    return pl.pallas_call(
        functools.partial(_mm_body, gelu_tiles=gelu_cols // tn),
        out_shape=jax.ShapeDtypeStruct((m, n), out_dtype),
        grid=(m // tm, n // tn),
        in_specs=[pl.BlockSpec((tm, k), lambda i, j: (i, 0)),
                  pl.BlockSpec((k, tn), lambda i, j: (0, j))],
        out_specs=pl.BlockSpec((tm, tn), lambda i, j: (i, j)),
        scratch_shapes=[pltpu.VMEM((tm, k), _BF16)],
        compiler_params=_params(("parallel", "arbitrary")),
        name=name,
    )(x, w)


def _mm_ln_body(*refs, n_a):
    a_refs = refs[:n_a]
    w_ref, r_ref, g_ref, b_ref, o_ref = refs[n_a:]
    acc = None
    row = 0
    for a_ref in a_refs:
        kk = a_ref.shape[1]
        part = jnp.dot(a_ref[...], w_ref[row:row + kk, :], preferred_element_type=_F32)
        acc = part if acc is None else acc + part
        row += kk
    o_ref[...] = _layernorm(ALPHA * r_ref[...] + acc, g_ref[...], b_ref[...])


def _mm_ln(a_list, w, resid, g, b, *, tm, name):
    m = resid.shape[0]
    n = w.shape[1]
    in_specs = [pl.BlockSpec((tm, a.shape[1]), lambda i: (i, 0)) for a in a_list]
    in_specs += [pl.BlockSpec(w.shape, lambda i: (0, 0)),
                 pl.BlockSpec((tm, n), lambda i: (i, 0)),
                 pl.BlockSpec((1, n), lambda i: (0, 0)),
                 pl.BlockSpec((1, n), lambda i: (0, 0))]
    return pl.pallas_call(
        functools.partial(_mm_ln_body, n_a=len(a_list)),
        out_shape=jax.ShapeDtypeStruct((m, n), _F32),
        grid=(m // tm,),
        in_specs=in_specs,
        out_specs=pl.BlockSpec((tm, n), lambda i: (i, 0)),
        compiler_params=_params(("parallel",)),
        name=name,
    )(*a_list, w, resid, g, b)


def _sgu_body(u_ref, gv_ref, lng_ref, lnb_ref, ws_ref, bs_ref, o_ref):
    tb = u_ref.shape[0]
    v = _layernorm(gv_ref[...].astype(_F32), lng_ref[...], lnb_ref[...]).astype(_BF16)
    r = lax.broadcasted_iota(jnp.int32, (A_BLOCK, A_BLOCK), 0) // CHUNK
    c = lax.broadcasted_iota(jnp.int32, (A_BLOCK, A_BLOCK), 1) // CHUNK
    causal = c <= r
    for g in range(A_GROUPS):
        wm = jnp.where(causal, ws_ref[g], 0.0).astype(_BF16)
        cols = slice(g * A_BLOCK, (g + 1) * A_BLOCK)
        for n in range(tb // A_BLOCK):
            rows = slice(n * A_BLOCK, (n + 1) * A_BLOCK)
            z = jnp.dot(wm, v[rows, cols], preferred_element_type=_F32) + bs_ref[g]
            o_ref[rows, cols] = (u_ref[rows, cols].astype(_F32) * z).astype(o_ref.dtype)


def _spatial_gating(h, ln_g, ln_b, w_s, b_s, *, tb=512):
    n = h.shape[0]
    return pl.pallas_call(
        _sgu_body,
        out_shape=jax.ShapeDtypeStruct((n, A_WIDTH), _BF16),
        grid=(n // tb,),
        in_specs=[pl.BlockSpec((tb, A_WIDTH), lambda i: (i, 0)),
                  pl.BlockSpec((tb, A_WIDTH), lambda i: (i, 1)),
                  pl.BlockSpec((1, A_WIDTH), lambda i: (0, 0)),
                  pl.BlockSpec((1, A_WIDTH), lambda i: (0, 0)),
                  pl.BlockSpec((A_GROUPS, A_BLOCK, A_BLOCK), lambda i: (0, 0, 0)),
                  pl.BlockSpec((A_GROUPS, A_BLOCK, 1), lambda i: (0, 0, 0))],
        out_specs=pl.BlockSpec((tb, A_WIDTH), lambda i: (i, 0)),
        compiler_params=_params(("parallel",)),
        name="spatial_gating",
    )(h, h, ln_g.reshape(1, A_WIDTH), ln_b.reshape(1, A_WIDTH), w_s, b_s.reshape(A_GROUPS, A_BLOCK, 1))


def _dattn_body(q_ref, k_ref, v_ref, slope_ref, lq1_ref, lk1_ref, lq2_ref, lk2_ref, sg_ref, o_ref,
                *scratch, tq, tk, hpg, lambda_init):
    m_ref, acc_ref, bias_ref, s0_ref, s1_ref, x0_ref, x1_ref = (
        scratch[k * hpg:(k + 1) * hpg] for k in range(7))
    s_ref, mx_ref = (s0_ref, s1_ref), (x0_ref, x1_ref)
    i = pl.program_id(2)
    rr = lax.broadcasted_iota(jnp.int32, (tq, tk), 0)
    cc = lax.broadcasted_iota(jnp.int32, (tq, tk), 1)
    rel = (rr - cc).astype(_F32)
    ones = jnp.ones((tk, LANES), _BF16)
    n_kv = ((i + 1) * tq + tk - 1) // tk
    last = n_kv - 1
    off_last = i * tq - last * tk
    lane = lax.broadcasted_iota(jnp.int32, (tq, LANES), 1)
    heads = range(hpg)
    slopes, q2s = [], []
    for a in heads:
        slope = slope_ref[a, :, 0:1]
        q = q_ref[:, a * LANES:(a + 1) * LANES]
        zero = jnp.zeros_like(q)
        q2 = jnp.concatenate([jnp.where(lane < DIFF_HEAD_DIM, q, zero),
                              jnp.where(lane >= DIFF_HEAD_DIM, q, zero)], axis=0)
        q2s.append(q2 * jnp.asarray(DIFF_HEAD_DIM ** -0.5, q2.dtype))
        slopes.append(slope)
        @pl.when(i == 0)
        def _(a=a, slope=slope):
            bias_ref[a][0] = -slope * rel
            for v in range(tk // tq):
                allowed = (cc // CHUNK) <= (rr // CHUNK) + (v * tq) // CHUNK
                bias_ref[a][1 + v] = jnp.where(allowed, -slope * jnp.abs(rel + float(v * tq)), NEG)

        m_ref[a][...] = jnp.full(m_ref[a].shape, NEG, _F32)
        acc_ref[a][...] = jnp.zeros(acc_ref[a].shape, _F32)

    def put_scores(a, j, slot):
        start = pl.multiple_of(j * tk, tk)
        kb = k_ref[pl.ds(start, tk), a * LANES:(a + 1) * LANES]
        bias = bias_ref[a][jnp.where(j == last, 1 + off_last // tq, 0)]
        s = lax.dot_general(q2s[a], kb, (((1,), (1,)), ((), ())), preferred_element_type=_F32)
        s = s + jnp.concatenate([bias, bias], axis=0)
        s_ref[slot][a][...] = s
        mx_ref[slot][a][...] = jnp.max(s, axis=-1, keepdims=True)

    def online_softmax(a, j, slot):
        off = jnp.where(j == last, 0, i * tq - j * tk)
        shift = -slopes[a] * jnp.full((1, 1), off, jnp.int32).astype(_F32)
        m = m_ref[a][...]
        m_new = jnp.maximum(m, mx_ref[slot][a][...] + shift)
        p = jnp.exp(s_ref[slot][a][...] - (m_new - shift))
        corr = jnp.exp(m - m_new)
        vb = v_ref[pl.ds(pl.multiple_of(j * tk, tk), tk), a * LANES:(a + 1) * LANES]
        v1 = jnp.concatenate([vb, ones], axis=1)
        acc_ref[a][...] = corr * acc_ref[a][...] + jnp.dot(p.astype(_BF16), v1, preferred_element_type=_F32)
        m_ref[a][...] = m_new

    def pipelined_block(j, slot, next_slot):
        for a in heads:
            online_softmax(a, j, slot)
            put_scores(a, j + 1, next_slot)

    def block_pair(t, c):
        pipelined_block(2 * t, 0, 1)
        pipelined_block(2 * t + 1, 1, 0)
        return c

    for a in heads:
        put_scores(a, 0, 0)
    lax.fori_loop(0, last // 2, block_pair, 0)

    @pl.when(last % 2 == 1)
    def _():
        pipelined_block(last - 1, 0, 0)

    lam = (jnp.exp(jnp.sum(lq1_ref[...] * lk1_ref[...], axis=-1, keepdims=True))
           - jnp.exp(jnp.sum(lq2_ref[...] * lk2_ref[...], axis=-1, keepdims=True)) + lambda_init)
    for a in heads:
        online_softmax(a, last, 0)
        acc = acc_ref[a][...]
        o = acc[:, :DIFF_V_DIM] / acc[:, DIFF_V_DIM:DIFF_V_DIM + 1]
        o = o[:tq] - lam * o[tq:]
        o = o * lax.rsqrt(jnp.mean(o * o, axis=-1, keepdims=True) + LN_EPS) * sg_ref[...]
        o_ref[:, a * DIFF_V_DIM:(a + 1) * DIFF_V_DIM] = (o * (1.0 - lambda_init)).astype(o_ref.dtype)


def _diff_attention(h, lq1, lk1, lq2, lk2, subln_g, lambda_init, *, batch, seq, tq=256, tk=512, hpg=2):
    n = h.shape[0]
    nq = seq // tq
    w = hpg * LANES
    qcol, kcol, vcol = (2 * A_GROUPS // hpg, (2 * A_GROUPS + DIFF_HEADS) // hpg,
                        (2 * A_GROUPS + 2 * DIFF_HEADS) // hpg)
    vec = lambda a: a.reshape(1, -1).astype(_F32)
    small = lambda ww: pl.BlockSpec((1, ww), lambda b, g, i: (0, 0))
    slopes = 2.0 ** (-8.0 * jnp.arange(1, DIFF_HEADS + 1, dtype=_F32) / DIFF_HEADS)
    slopes = jnp.broadcast_to(slopes[:, None, None], (DIFF_HEADS, 1, LANES))
    return pl.pallas_call(
        functools.partial(_dattn_body, tq=tq, tk=tk, hpg=hpg, lambda_init=lambda_init),
        out_shape=jax.ShapeDtypeStruct((n, DIFF_HEADS * DIFF_V_DIM), _BF16),
        grid=(batch, DIFF_HEADS // hpg, nq),
        in_specs=[pl.BlockSpec((tq, w), lambda b, g, i: (b * nq + i, qcol + g)),
                  pl.BlockSpec((seq, w), lambda b, g, i: (b, kcol + g)),
                  pl.BlockSpec((seq, w), lambda b, g, i: (b, vcol + g)),
                  pl.BlockSpec((hpg, 1, LANES), lambda b, g, i: (g, 0, 0)),
                  small(DIFF_HEAD_DIM), small(DIFF_HEAD_DIM), small(DIFF_HEAD_DIM), small(DIFF_HEAD_DIM),
                  small(DIFF_V_DIM)],
        out_specs=pl.BlockSpec((tq, hpg * DIFF_V_DIM), lambda b, g, i: (b * nq + i, g)),
        scratch_shapes=([pltpu.VMEM((2 * tq, 1), _F32)] * hpg
                        + [pltpu.VMEM((2 * tq, 2 * DIFF_V_DIM), _F32)] * hpg
                        + [pltpu.VMEM((1 + tk // tq, tq, tk), _F32)] * hpg
                        + [pltpu.VMEM((2 * tq, tk), _F32)] * (2 * hpg)
                        + [pltpu.VMEM((2 * tq, 1), _F32)] * (2 * hpg)),
        compiler_params=_params(("parallel", "parallel", "arbitrary")),
        name="diff_attention",
    )(h, h, h, slopes, vec(lq1), vec(lk1), vec(lq2), vec(lk2), vec(subln_g))


def _pool_body(h_ref, halo_ref, x_ref, wg_ref, sc_ref, wo_ref, g_ref, b_ref, o_ref, buf_ref, *, tiles_per_seq):
    t = h_ref.shape[0]
    it = pl.program_id(0) % tiles_per_seq
    halo = halo_ref[...]
    buf_ref[0:POOL_HALO, :] = jnp.where(it == 0, jnp.zeros_like(halo), halo)
    buf_ref[POOL_HALO:POOL_HALO + t, :] = h_ref[...]
    pos = it * t + lax.broadcasted_iota(jnp.int32, (t, 1), 0)
    acc = None
    for j, w in enumerate(POOL_WINDOWS):
        cols = slice(j * POOL_GROUP_CH, (j + 1) * POOL_GROUP_CH)
        hj = buf_ref[POOL_HALO:POOL_HALO + t, cols]
        tot = hj
        for s in range(1, w):
            tot = tot + buf_ref[POOL_HALO - s:POOL_HALO - s + t, cols]
        count = jnp.minimum(pos + 1, w).astype(_F32)
        pooled = tot / count - hj
        y = jnp.dot(pooled.astype(_BF16), wg_ref[j], preferred_element_type=_F32) * sc_ref[:, cols]
        part = jnp.dot(y.astype(_BF16), wo_ref[cols, :], preferred_element_type=_F32)
        acc = part if acc is None else acc + part
    o_ref[...] = _layernorm(ALPHA * x_ref[...] + acc, g_ref[...], b_ref[...])


def _pool_mixer(h, x, w_grp, scale, w_out, g, b, *, seq, t=256):
    n, d = h.shape
    tiles_per_seq = seq // t
    ratio = t // POOL_HALO
    return pl.pallas_call(
        functools.partial(_pool_body, tiles_per_seq=tiles_per_seq),
        out_shape=jax.ShapeDtypeStruct((n, d), _F32),
        grid=(n // t,),
        in_specs=[pl.BlockSpec((t, d), lambda i: (i, 0)),
                  pl.BlockSpec((POOL_HALO, d), lambda i: (jnp.maximum(i * ratio - 1, 0), 0)),
                  pl.BlockSpec((t, d), lambda i: (i, 0)),
                  pl.BlockSpec(w_grp.shape, lambda i: (0, 0, 0)),
                  pl.BlockSpec((1, d), lambda i: (0, 0)),
                  pl.BlockSpec(w_out.shape, lambda i: (0, 0)),
                  pl.BlockSpec((1, d), lambda i: (0, 0)),
                  pl.BlockSpec((1, d), lambda i: (0, 0))],
        out_specs=pl.BlockSpec((t, d), lambda i: (i, 0)),
        scratch_shapes=[pltpu.VMEM((POOL_HALO + t, d), _F32)],
        compiler_params=_params(("parallel",)),
        name="pool_mixer",
    )(h, h, x, w_grp, scale.reshape(1, d), w_out, g, b)


def _xattn_body(x_ref, wq_ref, k_ref, v_ref, wo_ref, g_ref, b_ref, o_ref):
    x = x_ref[...]
    q = jnp.dot(x.astype(_BF16), wq_ref[...], preferred_element_type=_F32).astype(_BF16)
    heads = []
    for hd in range(XA_HEADS):
        cols = slice(hd * XA_HEAD_DIM, (hd + 1) * XA_HEAD_DIM)
        s = lax.dot_general(q[:, cols], k_ref[:, cols], (((1,), (1,)), ((), ())),
                            preferred_element_type=_F32) * (XA_HEAD_DIM ** -0.5)
        s = s - jnp.max(s, axis=-1, keepdims=True)
        p = jnp.exp(s)
        p = p / jnp.sum(p, axis=-1, keepdims=True)
        heads.append(jnp.dot(p.astype(_BF16), v_ref[:, cols], preferred_element_type=_F32))
    o = jnp.concatenate(heads, axis=-1).astype(_BF16)
    c = jnp.dot(o, wo_ref[...], preferred_element_type=_F32)
    o_ref[...] = _layernorm(ALPHA * x + c, g_ref[...], b_ref[...])


def _cross_attention(x, kmem, vmem, w_q, w_o, g, b, *, seq, n_mem, tm=512):
    n, d = x.shape
    tiles_per_seq = seq // tm
    return pl.pallas_call(
        _xattn_body,
        out_shape=jax.ShapeDtypeStruct((n, d), _F32),
        grid=(n // tm,),
        in_specs=[pl.BlockSpec((tm, d), lambda i: (i, 0)),
                  pl.BlockSpec(w_q.shape, lambda i: (0, 0)),
                  pl.BlockSpec((n_mem, XA_WIDTH), lambda i: (i // tiles_per_seq, 0)),
                  pl.BlockSpec((n_mem, XA_WIDTH), lambda i: (i // tiles_per_seq, 0)),
                  pl.BlockSpec(w_o.shape, lambda i: (0, 0)),
                  pl.BlockSpec((1, d), lambda i: (0, 0)),
                  pl.BlockSpec((1, d), lambda i: (0, 0))],
        out_specs=pl.BlockSpec((tm, d), lambda i: (i, 0)),
        compiler_params=_params(("parallel",)),
        name="cross_attention",
    )(x, w_q, kmem, vmem, w_o, g, b)


def _router_body(x_ref, wr_ref, br_ref, idx_ref, gate_ref, rank_ref, cnt_ref, carry_ref):
    t = x_ref.shape[0]

    @pl.when(pl.program_id(0) == 0)
    def _():
        carry_ref[...] = jnp.zeros_like(carry_ref)

    def split(a):
        hi = a.astype(_BF16)
        return hi, (a - hi.astype(_F32)).astype(_BF16)

    def nt_dot(a, b):
        return lax.dot_general(a, b, (((1,), (1,)), ((), ())), preferred_element_type=_F32)

    w_hi, w_lo = split(wr_ref[...])
    x_hi, x_lo = split(x_ref[...])
    logits = nt_dot(w_hi, x_hi) + (nt_dot(w_hi, x_lo) + nt_dot(w_lo, x_hi)) + br_ref[...]
    e_iota = lax.broadcasted_iota(jnp.int32, (N_EXPERTS, t), 0).astype(_F32)
    work = logits
    vals, idxs, hots = [], [], []
    for _ in range(TOP_K):
        m = jnp.max(work, axis=0, keepdims=True)
        idx = jnp.min(jnp.where(work == m, e_iota, float(N_EXPERTS)), axis=0, keepdims=True)
        hot = e_iota == idx
        vals.append(m)
        idxs.append(idx)
        hots.append(hot)
        work = jnp.where(hot, -jnp.inf, work)
    exps = [jnp.exp(v - vals[0]) for v in vals]
    den = exps[0] + exps[1] + exps[2] + exps[3]
    cnt = sum(jnp.where(h, 1.0, 0.0) for h in hots)
    tr = lax.broadcasted_iota(jnp.int32, (t, t), 0)
    tc = lax.broadcasted_iota(jnp.int32, (t, t), 1)
    before = jnp.where(tr < tc, 1.0, 0.0).astype(_BF16)
    prior = carry_ref[...] + jnp.dot(cnt.astype(_BF16), before, preferred_element_type=_F32)
    for k in range(TOP_K):
        idx_ref[k:k + 1, :] = idxs[k].astype(jnp.int32)
        gate_ref[k:k + 1, :] = exps[k] / den
        rank_ref[k:k + 1, :] = jnp.sum(jnp.where(hots[k], prior, 0.0), axis=0, keepdims=True).astype(jnp.int32)
    total = carry_ref[...] + jnp.sum(cnt, axis=1, keepdims=True)
    carry_ref[...] = total
    cnt_ref[...] = jnp.broadcast_to(total, cnt_ref.shape).astype(jnp.int32)


def _router(x, wr_t, br, *, t=512):
    n, d = x.shape
    tok = lambda dt: jax.ShapeDtypeStruct((TOP_K, n), dt)
    tok_spec = pl.BlockSpec((TOP_K, t), lambda i: (0, i))
    return pl.pallas_call(
        _router_body,
        out_shape=(tok(jnp.int32), tok(_F32), tok(jnp.int32),
                   jax.ShapeDtypeStruct((N_EXPERTS, LANES), jnp.int32)),
        grid=(n // t,),
        in_specs=[pl.BlockSpec((t, d), lambda i: (i, 0)),
                  pl.BlockSpec((N_EXPERTS, d), lambda i: (0, 0)),
                  pl.BlockSpec((N_EXPERTS, 1), lambda i: (0, 0))],
        out_specs=(tok_spec, tok_spec, tok_spec, pl.BlockSpec((N_EXPERTS, LANES), lambda i: (0, 0))),
        scratch_shapes=[pltpu.VMEM((N_EXPERTS, 1), _F32)],
        compiler_params=_params(("arbitrary",)),
        name="router",
    )(x, wr_t, br)


PAD_CHUNKS = (128, 64, 32, 16, 8)


def _scatter_body(pad_start_ref, pad_len_ref, misc_ref, dest_ref, x_ref, rows_ref, zero_ref, stage_ref, sem, zsem):
    t = x_ref.shape[0]
    n_rows = rows_ref.shape[0]
    zrows = zero_ref.shape[0]

    @pl.when(pl.program_id(0) == 0)
    def _():
        zero_ref[...] = jnp.zeros(zero_ref.shape, zero_ref.dtype)

        def pad_copies(e, wait):
            start = pad_start_ref[e]
            length = pad_len_ref[e]
            end = start + length
            done = 0
            for chunk in PAD_CHUNKS:
                done = done + (length & chunk)
                pos = pl.multiple_of(end - done, SUBLANES)
                cp = pltpu.make_async_copy(zero_ref.at[pl.ds(0, chunk), :],
                                           rows_ref.at[pl.ds(pos, chunk), :], zsem)

                @pl.when((length & chunk) != 0)
                def _():
                    cp.wait() if wait else cp.start()

            for r in range(SUBLANES - 1):
                cp = pltpu.make_async_copy(zero_ref.at[pl.ds(0, 1), :],
                                           rows_ref.at[pl.ds(start + r, 1), :], zsem)

                @pl.when(r < (length & (SUBLANES - 1)))
                def _():
                    cp.wait() if wait else cp.start()

        def tail_copy(bb):
            return pltpu.make_async_copy(zero_ref, rows_ref.at[pl.ds(bb * zrows, zrows), :], zsem)

        tail0 = misc_ref[0] * (MOE_BLOCK // zrows)
        for wait in (False, True):
            lax.fori_loop(0, N_EXPERTS, lambda e, c: (pad_copies(e, wait), c)[1], 0)
            lax.fori_loop(tail0, n_rows // zrows,
                          lambda bb, c: ((tail_copy(bb).wait() if wait else tail_copy(bb).start()), c)[1], 0)

    i = pl.program_id(0)
    slot = i % 2

    def drain(sl):
        for k in range(TOP_K):
            pltpu.make_async_copy(stage_ref.at[sl], rows_ref.at[pl.ds(0, t), :], sem.at[sl]).wait()

    @pl.when(i >= 2)
    def _():
        drain(slot)

    stage_ref[slot] = x_ref[...]

    def issue(tt, c):
        for k in range(TOP_K):
            pltpu.make_async_copy(stage_ref.at[slot, pl.ds(tt, 1), :],
                                  rows_ref.at[pl.ds(dest_ref[0, k, tt], 1), :],
                                  sem.at[slot]).start(priority=k % DMA_PRIORITIES)
        return c

    lax.fori_loop(0, t, issue, 0, unroll=4)

    @pl.when(i == pl.num_programs(0) - 1)
    def _():
        drain(slot)

        @pl.when(i >= 1)
        def _():
            drain(1 - slot)


def _scatter_rows(x, dest3, pad_start, pad_len, misc, n_rows, *, t=256):
    n, d = x.shape
    grid_spec = pltpu.PrefetchScalarGridSpec(
        num_scalar_prefetch=3,
        grid=(n // t,),
        in_specs=[pl.BlockSpec((1, TOP_K, t), lambda i, *_: (i, 0, 0), memory_space=pltpu.SMEM),
                  pl.BlockSpec((t, d), lambda i, *_: (i, 0))],
        out_specs=pl.BlockSpec(memory_space=pl.ANY),
        scratch_shapes=[pltpu.VMEM((PAD_CHUNKS[0], d), x.dtype),
                        pltpu.VMEM((2, t, d), x.dtype),
                        pltpu.SemaphoreType.DMA((2,)), pltpu.SemaphoreType.DMA],
    )
    return pl.pallas_call(
        _scatter_body,
        out_shape=jax.ShapeDtypeStruct((n_rows, d), x.dtype),
        grid_spec=grid_spec,
        compiler_params=_params(("arbitrary",)),
        name="moe_scatter",
    )(pad_start, pad_len, misc, dest3, x)


def _combine_body(dest_ref, next_dest_ref, y_ref, gate_ref, x_ref, g_ref, b_ref, o_ref, buf_ref, sem):
    t = x_ref.shape[0]
    i = pl.program_id(0)
    slot = i % 2

    def gather(idx_ref, sl):
        def issue(tt, c):
            for k in range(TOP_K):
                pltpu.make_async_copy(y_ref.at[pl.ds(idx_ref[0, k, tt], 1), :],
                                      buf_ref.at[sl, k, pl.ds(tt, 1), :],
                                      sem.at[sl]).start(priority=k % DMA_PRIORITIES)
            return c

        lax.fori_loop(0, t, issue, 0, unroll=4)

    @pl.when(i == 0)
    def _():
        gather(dest_ref, slot)

    @pl.when(i + 1 < pl.num_programs(0))
    def _():
        gather(next_dest_ref, 1 - slot)

    for k in range(TOP_K):
        pltpu.make_async_copy(y_ref.at[pl.ds(0, t), :], buf_ref.at[slot, k], sem.at[slot]).wait()
    gate = gate_ref[...]
    f = gate[:, 0:1] * buf_ref[slot, 0]
    for k in range(1, TOP_K):
        f = f + gate[:, k:k + 1] * buf_ref[slot, k]
    o_ref[...] = _layernorm(ALPHA * x_ref[...] + f, g_ref[...], b_ref[...])


def _combine(y_rows, dest3, gates, x, g, b, *, t=256):
    n, d = x.shape
    return pl.pallas_call(
        _combine_body,
        out_shape=jax.ShapeDtypeStruct((n, d), _F32),
        grid=(n // t,),
        in_specs=[pl.BlockSpec((1, TOP_K, t), lambda i: (i, 0, 0), memory_space=pltpu.SMEM),
                  pl.BlockSpec((1, TOP_K, t), lambda i: (jnp.minimum(i + 1, n // t - 1), 0, 0),
                               memory_space=pltpu.SMEM),
                  pl.BlockSpec(memory_space=pl.ANY),
                  pl.BlockSpec((t, TOP_K), lambda i: (i, 0)),
                  pl.BlockSpec((t, d), lambda i: (i, 0)),
                  pl.BlockSpec((1, d), lambda i: (0, 0)),
                  pl.BlockSpec((1, d), lambda i: (0, 0))],
        out_specs=pl.BlockSpec((t, d), lambda i: (i, 0)),
        scratch_shapes=[pltpu.VMEM((2, TOP_K, t, d), _F32), pltpu.SemaphoreType.DMA((2,))],
        compiler_params=_params(("arbitrary",)),
        name="moe_combine",
    )(dest3, dest3, y_rows, gates, x, g, b)


def _moe_body(e_ref, blk0_ref, nblk_ref, misc_ref,
              xr_ref, wg_ref, wl_ref, bg_ref, bl_ref, wd_ref, bd_ref, y_ref,
              xs_ref, acc_ref, stage_ref, wgl_ref, wds_ref, in_sem, out_sem):
    del e_ref
    s = pl.program_id(0)
    f = pl.program_id(1)
    n_s = pl.num_programs(0)
    nblk = nblk_ref[s]
    d = acc_ref.shape[1]

    def block_rows(j):
        return pl.ds(pl.multiple_of(j * MOE_BLOCK, MOE_BLOCK), MOE_BLOCK)

    def in_copy(sb, j):
        return pltpu.make_async_copy(xr_ref.at[blk0_ref[sb] + j], stage_ref.at[j], in_sem.at[j])

    def out_copy(sb, j):
        return pltpu.make_async_copy(acc_ref.at[block_rows(j), :], y_ref.at[blk0_ref[sb] + j], out_sem)

    def for_blocks(n, fn):
        lax.fori_loop(0, n, lambda j, c: (fn(j), c)[1], 0)

    @pl.when(jnp.logical_and(s == 0, f == 0))
    def _():
        for_blocks(nblk, lambda j: in_copy(0, j).start())

    @pl.when(f == 0)
    def _():
        def take(j):
            in_copy(s, j).wait()
            xs_ref[block_rows(j), :] = stage_ref[j].astype(_BF16)

        for_blocks(nblk, take)

        @pl.when(s > 0)
        def _():
            prev = jnp.maximum(s - 1, 0)
            for_blocks(nblk_ref[prev], lambda j: out_copy(prev, j).wait())

        def reset(j):
            acc_ref[block_rows(j), :] = jnp.broadcast_to(bd_ref[...], (MOE_BLOCK, d))

        for_blocks(nblk, reset)

    @pl.when(jnp.logical_and(f == 1, s + 1 < n_s))
    def _():
        nxt = jnp.minimum(s + 1, n_s - 1)
        for_blocks(nblk_ref[nxt], lambda j: in_copy(nxt, j).start())

    @pl.when(nblk > 0)
    def _():
        wgl_ref[:, :MOE_TF] = wg_ref[...].astype(_BF16)
        wgl_ref[:, MOE_TF:] = wl_ref[...].astype(_BF16)
        wds_ref[...] = wd_ref[...].astype(_BF16)

        def rows_chunk(row0, m):
            rows = pl.ds(pl.multiple_of(row0, MOE_BLOCK), m)
            h = jnp.dot(xs_ref[rows, :], wgl_ref[...], preferred_element_type=_F32)
            glu = h[:, :MOE_TF] + bg_ref[...]
            lin = h[:, MOE_TF:] + bl_ref[...]
            glu = jnp.minimum(glu, SWIGLU_LIMIT)
            lin = jnp.clip(lin, -SWIGLU_LIMIT, SWIGLU_LIMIT)
            act = (glu * jax.nn.sigmoid(SWIGLU_ALPHA * glu) * (lin + 1.0)).astype(_BF16)
            for c0 in range(0, d, MOE_DOWN_COLS):
                cols = slice(c0, c0 + MOE_DOWN_COLS)
                acc_ref[rows, cols] += jnp.dot(act, wds_ref[:, cols], preferred_element_type=_F32)

                @pl.when(f == MOE_NF - 1)
                def _(c0=c0):
                    for b in range(m // MOE_BLOCK):
                        r0 = pl.multiple_of(row0 + b * MOE_BLOCK, MOE_BLOCK)
                        pltpu.make_async_copy(
                            acc_ref.at[pl.ds(r0, MOE_BLOCK), pl.ds(c0, MOE_DOWN_COLS)],
                            y_ref.at[blk0_ref[s] + row0 // MOE_BLOCK + b, :, pl.ds(c0, MOE_DOWN_COLS)],
                            out_sem).start()

        big = MOE_CHUNK_BLOCKS * MOE_BLOCK
        lax.fori_loop(0, nblk // MOE_CHUNK_BLOCKS,
                      lambda c, carry: (rows_chunk(c * big, big), carry)[1], 0)
        part = MOE_CHUNK_BLOCKS // 2
        while part >= 1:
            @pl.when((nblk & part) != 0)
            def _(part=part):
                rows_chunk((nblk & ~(2 * part - 1)) * MOE_BLOCK, part * MOE_BLOCK)

            part //= 2

    @pl.when(jnp.logical_and(f == MOE_NF - 1, s == n_s - 1))
    def _():
        for_blocks(nblk, lambda j: out_copy(s, j).wait())
        stage_ref[0] = jnp.zeros(stage_ref.shape[1:], stage_ref.dtype)

        def fill(bb):
            cp = pltpu.make_async_copy(stage_ref.at[0], y_ref.at[bb], in_sem.at[0])
            cp.start()
            cp.wait()

        lax.fori_loop(misc_ref[0], y_ref.shape[0], lambda bb, c: (fill(bb), c)[1], 0)


def _moe_experts(x_rows, sched, w_gu, b_gu, w_down, b_down, layer):
    n_blocks = x_rows.shape[0] // MOE_BLOCK
    d = x_rows.shape[1]
    xr3 = x_rows.reshape(n_blocks, MOE_BLOCK, d)
    sb_e, sb_blk0, sb_nblk, misc = sched
    s_max = sb_e.shape[0]

    def f_eff(s, f, nb):
        return jnp.where(nb[s] > 0, f, MOE_NF - 1)

    wg_map = lambda s, f, e, b0, nb, mi: (layer, e[s], 0, f_eff(s, f, nb))
    wl_map = lambda s, f, e, b0, nb, mi: (layer, e[s], 0, MOE_NF + f_eff(s, f, nb))
    wd_map = lambda s, f, e, b0, nb, mi: (layer, e[s], f_eff(s, f, nb), 0)
    bd_map = lambda s, f, e, b0, nb, mi: (layer, e[s], 0, 0)
    grid_spec = pltpu.PrefetchScalarGridSpec(
        num_scalar_prefetch=4,
        grid=(s_max, MOE_NF),
        in_specs=[pl.BlockSpec(memory_space=pl.ANY),
                  pl.BlockSpec((None, None, d, MOE_TF), wg_map),
                  pl.BlockSpec((None, None, d, MOE_TF), wl_map),
                  pl.BlockSpec((None, None, 1, MOE_TF), wg_map),
                  pl.BlockSpec((None, None, 1, MOE_TF), wl_map),
                  pl.BlockSpec((None, None, MOE_TF, d), wd_map),
                  pl.BlockSpec((None, None, 1, d), bd_map)],
        out_specs=pl.BlockSpec(memory_space=pl.ANY),
        scratch_shapes=[pltpu.VMEM((MOE_SB_BLOCKS * MOE_BLOCK, d), _BF16),
                        pltpu.VMEM((MOE_SB_BLOCKS * MOE_BLOCK, d), _F32),
                        pltpu.VMEM((MOE_SB_BLOCKS, MOE_BLOCK, d), _F32),
                        pltpu.VMEM((d, 2 * MOE_TF), _BF16),
                        pltpu.VMEM((MOE_TF, d), _BF16),
                        pltpu.SemaphoreType.DMA((MOE_SB_BLOCKS,)),
                        pltpu.SemaphoreType.DMA],
    )
    b_gu4 = b_gu.reshape(DEPTH, N_EXPERTS, 1, 2 * D_FF)
    b_down4 = b_down.reshape(DEPTH, N_EXPERTS, 1, d)
    y3 = pl.pallas_call(
        _moe_body,
        out_shape=jax.ShapeDtypeStruct((n_blocks, MOE_BLOCK, d), _F32),
        grid_spec=grid_spec,
        compiler_params=_params(("arbitrary", "arbitrary"), vmem=MOE_VMEM_LIMIT_BYTES),
        name="moe_experts",
    )(sb_e, sb_blk0, sb_nblk, misc, xr3, w_gu, w_gu, b_gu4, b_gu4, w_down, b_down4)
    return y3.reshape(n_blocks * MOE_BLOCK, d)


def _moe_schedule(counts, s_max):
    nblk_e = (counts + MOE_BLOCK - 1) // MOE_BLOCK
    blk_start = jnp.cumsum(nblk_e) - nblk_e
    ns_e = (nblk_e + MOE_SB_BLOCKS - 1) // MOE_SB_BLOCKS
    cum_ns = jnp.cumsum(ns_e)
    total_s = cum_ns[-1]
    s = jnp.arange(s_max, dtype=jnp.int32)
    valid = s < total_s
    e_of = lambda v: jnp.minimum(jnp.sum(cum_ns[None, :] <= jnp.reshape(v, (-1, 1)), axis=1),
                                 N_EXPERTS - 1).astype(jnp.int32)
    e = jnp.where(valid, e_of(s), e_of(jnp.maximum(total_s - 1, 0)))
    within = s - (cum_ns[e] - ns_e[e])
    blk0 = jnp.where(valid, blk_start[e] + within * MOE_SB_BLOCKS, 0)
    nblk = jnp.where(valid, jnp.minimum(MOE_SB_BLOCKS, nblk_e[e] - within * MOE_SB_BLOCKS), 0)
    misc = jnp.sum(nblk_e).reshape(1)
    i32 = lambda a: a.astype(jnp.int32)
    row_start = blk_start * MOE_BLOCK
    pads = (i32(row_start + counts), i32(nblk_e * MOE_BLOCK - counts))
    return (i32(e), i32(blk0), i32(nblk), i32(misc)), i32(row_start), pads


def _moe_layer(x, w_router, b_router, w_gu, b_gu, w_down, b_down, g, b, layer, *, t=256):
    n, d = x.shape
    n_assign = n * TOP_K
    n_rows = -(-n_assign // MOE_BLOCK) * MOE_BLOCK + N_EXPERTS * MOE_BLOCK
    n_blocks = n_rows // MOE_BLOCK
    s_max = (n_blocks + N_EXPERTS * (MOE_SB_BLOCKS - 1)) // MOE_SB_BLOCKS + 1
    idx, gates, rank, cnt = _router(x, w_router.T, b_router.reshape(N_EXPERTS, 1))
    sched, row_start, (pad_start, pad_len) = _moe_schedule(cnt[:, 0], s_max)
    hot = idx[None] == jnp.arange(N_EXPERTS, dtype=jnp.int32)[:, None, None]
    dest = jnp.sum(jnp.where(hot, row_start[:, None, None], 0), axis=0) + rank
    dest3 = dest.reshape(TOP_K, n // t, t).transpose(1, 0, 2)
    x_rows = _scatter_rows(x, dest3, pad_start, pad_len, sched[3], n_rows, t=t)
    y_rows = _moe_experts(x_rows, sched, w_gu, b_gu, w_down, b_down, layer)
    return _combine(y_rows, dest3, gates.T, x, g, b, t=t)


def kernel(x, mem, even_w_in, even_ln_v_g, even_ln_v_b, even_w_s, even_b_s, even_lam_q1, even_lam_k1, even_lam_q2, even_lam_k2, even_subln_g, even_w_out, odd_w_in, odd_w_grp, odd_scale, odd_w_out, xa_w_q, xa_w_k, xa_w_v, xa_w_o, moe_w_router, moe_b_router, moe_w_gu, moe_b_gu, moe_w_down, moe_b_down, ln_g, ln_b):
    batch, seq, d = x.shape
    n_mem = mem.shape[1]
    xt = x.reshape(batch * seq, d)
    memt = mem.reshape(batch * n_mem, d)
    bf = lambda a: a.astype(_BF16)
    ln = lambda l, j: (ln_g[l, j].reshape(1, d), ln_b[l, j].reshape(1, d))
    for l in range(DEPTH):
        i = l // 2
        if l % 2 == 0:
            lambda_init = 0.8 - 0.6 * math.exp(-0.3 * l)
            h = _mm(xt, bf(even_w_in[i]), tm=1024, tn=1024, out_dtype=_BF16, gelu_cols=2 * A_WIDTH,
                    name="even_in_proj")
            a_out = _spatial_gating(h, even_ln_v_g[i], even_ln_v_b[i], even_w_s[i], even_b_s[i])
            b_out = _diff_attention(h, even_lam_q1[i], even_lam_k1[i], even_lam_q2[i], even_lam_k2[i],
                                    even_subln_g[i], lambda_init, batch=batch, seq=seq)
            xt = _mm_ln([a_out, b_out], bf(even_w_out[i]), xt, *ln(l, 0), tm=512, name="even_out_proj")
        else:
            h = _mm(xt, bf(odd_w_in[i]), tm=1024, tn=1024, out_dtype=_F32, name="odd_in_proj")
            xt = _pool_mixer(h, xt, bf(odd_w_grp[i]), odd_scale[i], bf(odd_w_out[i]), *ln(l, 0), seq=seq)
        kmem = _mm(memt, bf(xa_w_k[l]), tm=memt.shape[0], tn=XA_WIDTH, out_dtype=_BF16, name="xattn_k_proj")
        vmem = _mm(memt, bf(xa_w_v[l]), tm=memt.shape[0], tn=XA_WIDTH, out_dtype=_BF16, name="xattn_v_proj")
        xt = _cross_attention(xt, kmem, vmem, bf(xa_w_q[l]), bf(xa_w_o[l]), *ln(l, 1), seq=seq, n_mem=n_mem)
        xt = _moe_layer(xt, moe_w_router[l], moe_b_router[l], moe_w_gu, moe_b_gu, moe_w_down, moe_b_down,
                        *ln(l, 2), l)
    return xt.reshape(batch, seq, d)
```
